```python
import math
import jax, jax.numpy as jnp
from jax import lax
import numpy as np

D_MODEL = 2048
BATCH = 1
SEQ = 8192
DEPTH = 4
DEC_BATCH = 8
DEC_SEQ = 2048
PAST_LEN = 128

N_EVEN = (DEPTH + 1) // 2
N_ODD = DEPTH // 2
EPS = 1e-6
D_FF = 5632
POOL_WIDTH = D_MODEL // 2
POOL_WINDOWS = (2, 4, 8, 16)
N_POOL_GROUPS = len(POOL_WINDOWS)
POOL_GROUP = POOL_WIDTH // N_POOL_GROUPS
HYENA_WIDTH = D_MODEL - POOL_WIDTH
HYENA_EMB_BANDS = 16
HYENA_EMB_DIM = 1 + 2 * HYENA_EMB_BANDS
HYENA_FILTER_ORDER = 64
HYENA_FAST_DECAY = 0.3
HYENA_SLOW_DECAY = 1.5
HYENA_TARGET = 1e-2
AB_IN = POOL_WIDTH + 3 * HYENA_WIDTH
N_HEADS = 16
N_KV_HEADS = 4
HEAD_DIM = 128
GQA_GROUP = N_HEADS // N_KV_HEADS
WINDOW = 128
BLOCK = 128
QKV_OUT = (N_HEADS + 2 * N_KV_HEADS) * HEAD_DIM
N_BUCKETS = 32
MAX_DISTANCE = 128

kernel_name = "hybrid_pool_hyena_swa_encoder"


def rmsnorm(x, g):
    xf = x.astype(jnp.float32)
    y = xf * lax.rsqrt(jnp.mean(xf * xf, axis=-1, keepdims=True) + EPS)
    return (y * g.astype(jnp.float32)).astype(x.dtype)


def swiglu(h, w_in, w_out):
    gate, up = jnp.split(h @ w_in, 2, axis=-1)
    return (jax.nn.silu(gate) * up) @ w_out


def multiscale_pool(u, w_grp, scale):
    B, L, _ = u.shape
    uf = u.astype(jnp.float32)
    cs = jnp.concatenate([jnp.zeros((B, 1, POOL_WIDTH), jnp.float32), jnp.cumsum(uf, axis=1)], axis=1)
    t = jnp.arange(L)
    outs = []
    for g, w in enumerate(POOL_WINDOWS):
        sl = slice(g * POOL_GROUP, (g + 1) * POOL_GROUP)
        lo = jnp.clip(t - w // 2, 0, L)
        hi = jnp.clip(t + w // 2, 0, L)
        csg = cs[..., sl]
        s = jnp.take(csg, hi, axis=1) - jnp.take(csg, lo, axis=1)
        mean = s / (hi - lo).astype(jnp.float32)[None, :, None]
        outs.append(mean - uf[..., sl])
    p = jnp.stack(outs, axis=2).astype(u.dtype)
    y = jnp.einsum('blgc,gcd->blgd', p, w_grp).reshape(B, L, POOL_WIDTH)
    return y * scale


def hyena_filters(L, w1, b1, w2, b2, w3, b3, w4, b4, freq):
    f32 = jnp.float32
    t = jnp.linspace(0.0, 1.0, L, dtype=f32)[:, None]
    w = 2.0 * math.pi * jnp.arange(L, dtype=f32)[:, None] / L
    f = jnp.linspace(1e-4, HYENA_EMB_BANDS - 1, HYENA_EMB_BANDS, dtype=f32)[None, :]
    z = jnp.concatenate([t, jnp.cos(f * w), -jnp.sin(f * w)], axis=-1)
    fr = freq.astype(f32)
    h = jnp.sin(fr * (z @ w1.astype(f32) + b1.astype(f32)))
    h = jnp.sin(fr * (h @ w2.astype(f32) + b2.astype(f32)))
    h = jnp.sin(fr * (h @ w3.astype(f32) + b3.astype(f32)))
    h = h @ w4.astype(f32) + b4.astype(f32)
    min_decay = math.log(HYENA_TARGET) / HYENA_SLOW_DECAY
    max_decay = math.log(HYENA_TARGET) / HYENA_FAST_DECAY
    deltas = jnp.linspace(min_decay, max_decay, HYENA_WIDTH, dtype=f32)
    decay = jnp.exp(-t * jnp.abs(deltas)[None, :])
    h_f = h[:, :HYENA_WIDTH] * decay
    h_b = h[:, HYENA_WIDTH:] * decay
    k = jnp.concatenate([h_f, jnp.zeros((1, HYENA_WIDTH), f32), h_b[:0:-1]], axis=0)
    return k / jnp.sum(jnp.abs(k), axis=0, keepdims=True)


def hyena(u, conv_w, conv_b, w1, b1, w2, b2, w3, b3, w4, b4, freq, d_bias):
    B, L, _ = u.shape
    up = jnp.pad(u, ((0, 0), (1, 1), (0, 0)))
    uc = up[:, :-2] * conv_w[0] + up[:, 1:-1] * conv_w[1] + up[:, 2:] * conv_w[2] + conv_b
    x0, x1, v = jnp.split(uc, 3, axis=-1)
    k = hyena_filters(L, w1, b1, w2, b2, w3, b3, w4, b4, freq)
    v = (v * x1).astype(jnp.float32)
    vf = jnp.fft.rfft(v, n=2 * L, axis=1)
    kf = jnp.fft.rfft(k, axis=0)
    y = jnp.fft.irfft(vf * kf[None], n=2 * L, axis=1)[:, :L] + v * d_bias.astype(jnp.float32)
    return (y * x0.astype(jnp.float32)).astype(u.dtype)


def t5_buckets(rel):
    nb = N_BUCKETS // 2
    max_exact = nb // 2
    ret = (rel > 0).astype(jnp.int32) * nb
    n = jnp.abs(rel)
    large = max_exact + (jnp.log(jnp.maximum(n, 1).astype(jnp.float32) / max_exact)
                         / math.log(MAX_DISTANCE / max_exact) * (nb - max_exact)).astype(jnp.int32)
    large = jnp.minimum(large, nb - 1)
    return ret + jnp.where(n < max_exact, n, large)


def windowed_gqa(h, w_qkv, w_o, sink, rel_bias):
    B, L, _ = h.shape
    nb = L // BLOCK
    q, k, v = jnp.split(h @ w_qkv, [N_HEADS * HEAD_DIM, (N_HEADS + N_KV_HEADS) * HEAD_DIM], axis=-1)
    q = q.reshape(B, nb, BLOCK, N_KV_HEADS, GQA_GROUP, HEAD_DIM)

    def band(t):
        t = t.reshape(B, L, N_KV_HEADS, HEAD_DIM)
        t = jnp.pad(t, ((0, 0), (BLOCK, BLOCK), (0, 0), (0, 0))).reshape(B, nb + 2, BLOCK, N_KV_HEADS, HEAD_DIM)
        return jnp.concatenate([t[:, :-2], t[:, 1:-1], t[:, 2:]], axis=2)

    kb, vb = band(k), band(v)
    s = jnp.einsum('bnqkgd,bnpkd->bnkgqp', q, kb).astype(jnp.float32) / math.sqrt(HEAD_DIM)
    qi = jnp.arange(BLOCK)[:, None]
    pj = jnp.arange(3 * BLOCK)[None, :] - BLOCK
    rel = pj - qi
    bias = rel_bias.astype(jnp.float32)[t5_buckets(rel)]
    bias = bias.transpose(2, 0, 1).reshape(N_KV_HEADS, GQA_GROUP, BLOCK, 3 * BLOCK)
    kpos = jnp.arange(nb)[:, None] * BLOCK + pj
    valid = (jnp.abs(rel) <= WINDOW)[None] & ((kpos >= 0) & (kpos < L))[:, None, :]
    s = jnp.where(valid[None, :, None, None], s + bias, -jnp.inf)
    sink_l = sink.astype(jnp.float32).reshape(1, 1, N_KV_HEADS, GQA_GROUP, 1, 1)
    m = jnp.maximum(jnp.max(s, axis=-1, keepdims=True), sink_l)
    p = jnp.exp(s - m)
    p = p / (jnp.sum(p, axis=-1, keepdims=True) + jnp.exp(sink_l - m))
    o = jnp.einsum('bnkgqp,bnpkd->bnqkgd', p.astype(h.dtype), vb).reshape(B, L, N_HEADS * HEAD_DIM)
    return o @ w_o


def trunk(x, p):
    for layer in range(DEPTH):
        x = x + 0.5 * swiglu(rmsnorm(x, p['norm_ffn1'][layer]), p['ffn1_wi'][layer], p['ffn1_wo'][layer])
        h = rmsnorm(x, p['norm_mix'][layer])
        i = layer // 2
        if layer % 2 == 0:
            u = h @ p['ab_w_in'][i]
            ua, ub = u[..., :POOL_WIDTH], u[..., POOL_WIDTH:]
            ya = multiscale_pool(ua, p['pool_w'][i], p['pool_scale'][i])
            yb = hyena(ub, p['hy_conv_w'][i], p['hy_conv_b'][i],
                       p['hy_ff_w1'][i], p['hy_ff_b1'][i], p['hy_ff_w2'][i], p['hy_ff_b2'][i],
                       p['hy_ff_w3'][i], p['hy_ff_b3'][i], p['hy_ff_w4'][i], p['hy_ff_b4'][i],
                       p['hy_freq'][i], p['hy_d'][i])
            x = x + jnp.concatenate([ya, yb], axis=-1) @ p['ab_w_out'][i]
        else:
            x = x + windowed_gqa(h, p['attn_w_qkv'][i], p['attn_w_o'][i], p['attn_sink'][i], p['rel_bias'])
        x = x + 0.5 * swiglu(rmsnorm(x, p['norm_ffn2'][layer]), p['ffn2_wi'][layer], p['ffn2_wo'][layer])
    return rmsnorm(x, p['norm_final'])


def setup_inputs(seed: int = 0) -> dict:
    key = jax.random.key(seed)
    ks = jax.random.split(key, 32)
    f32 = jnp.float32

    def nrm(k, shape, scale):
        return jax.random.normal(k, shape, f32) * scale

    def gain(k, shape):
        return 1.0 + 0.05 * jax.random.normal(k, shape, f32)

    return {
        'x_prompt': nrm(ks[0], (BATCH, SEQ, D_MODEL), 1.0),
        'x_sample': nrm(ks[1], (DEC_BATCH, DEC_SEQ, D_MODEL), 1.0),
        'norm_ffn1': gain(ks[2], (DEPTH, D_MODEL)),
        'ffn1_wi': nrm(ks[3], (DEPTH, D_MODEL, 2 * D_FF), D_MODEL ** -0.5),
        'ffn1_wo': nrm(ks[4], (DEPTH, D_FF, D_MODEL), D_FF ** -0.5),
        'norm_mix': gain(ks[5], (DEPTH, D_MODEL)),
        'ab_w_in': nrm(ks[6], (N_EVEN, D_MODEL, AB_IN), D_MODEL ** -0.5),
        'pool_w': nrm(ks[7], (N_EVEN, N_POOL_GROUPS, POOL_GROUP, POOL_GROUP), POOL_GROUP ** -0.5),
        'pool_scale': 1.0 + 0.1 * jax.random.normal(ks[8], (N_EVEN, POOL_WIDTH), f32),
        'hy_conv_w': nrm(ks[9], (N_EVEN, 3, 3 * HYENA_WIDTH), 3 ** -0.5),
        'hy_conv_b': nrm(ks[10], (N_EVEN, 3 * HYENA_WIDTH), 0.02),
        'hy_ff_w1': nrm(ks[11], (N_EVEN, HYENA_EMB_DIM, HYENA_FILTER_ORDER), HYENA_EMB_DIM ** -0.5),
        'hy_ff_b1': nrm(ks[12], (N_EVEN, HYENA_FILTER_ORDER), 0.1),
        'hy_ff_w2': nrm(ks[13], (N_EVEN, HYENA_FILTER_ORDER, HYENA_FILTER_ORDER), HYENA_FILTER_ORDER ** -0.5),
        'hy_ff_b2': nrm(ks[14], (N_EVEN, HYENA_FILTER_ORDER), 0.1),
        'hy_ff_w3': nrm(ks[15], (N_EVEN, HYENA_FILTER_ORDER, HYENA_FILTER_ORDER), HYENA_FILTER_ORDER ** -0.5),
        'hy_ff_b3': nrm(ks[16], (N_EVEN, HYENA_FILTER_ORDER), 0.1),
        'hy_ff_w4': nrm(ks[17], (N_EVEN, HYENA_FILTER_ORDER, 2 * HYENA_WIDTH), HYENA_FILTER_ORDER ** -0.5),
        'hy_ff_b4': nrm(ks[18], (N_EVEN, 2 * HYENA_WIDTH), 0.1),
        'hy_freq': 1.0 + 0.1 * jax.random.normal(ks[19], (N_EVEN, HYENA_FILTER_ORDER), f32),
        'hy_d': nrm(ks[20], (N_EVEN, HYENA_WIDTH), 1.0),
        'ab_w_out': nrm(ks[21], (N_EVEN, D_MODEL, D_MODEL), D_MODEL ** -0.5),
        'attn_w_qkv': nrm(ks[22], (N_ODD, D_MODEL, QKV_OUT), D_MODEL ** -0.5),
        'attn_w_o': nrm(ks[23], (N_ODD, N_HEADS * HEAD_DIM, D_MODEL), (N_HEADS * HEAD_DIM) ** -0.5),
        'attn_sink': nrm(ks[24], (N_ODD, N_HEADS), 1.0),
        'rel_bias': nrm(ks[25], (N_BUCKETS, N_HEADS), 0.5),
        'norm_ffn2': gain(ks[26], (DEPTH, D_MODEL)),
        'ffn2_wi': nrm(ks[27], (DEPTH, D_MODEL, 2 * D_FF), D_MODEL ** -0.5),
        'ffn2_wo': nrm(ks[28], (DEPTH, D_FF, D_MODEL), D_FF ** -0.5),
        'norm_final': gain(ks[29], (D_MODEL,)),
    }


def reference(x_prompt, x_sample, norm_ffn1, ffn1_wi, ffn1_wo, norm_mix, ab_w_in, pool_w, pool_scale,
              hy_conv_w, hy_conv_b, hy_ff_w1, hy_ff_b1, hy_ff_w2, hy_ff_b2, hy_ff_w3, hy_ff_b3,
              hy_ff_w4, hy_ff_b4, hy_freq, hy_d, ab_w_out, attn_w_qkv, attn_w_o, attn_sink, rel_bias,
              norm_ffn2, ffn2_wi, ffn2_wo, norm_final):
    p = dict(norm_ffn1=norm_ffn1, ffn1_wi=ffn1_wi, ffn1_wo=ffn1_wo, norm_mix=norm_mix,
             ab_w_in=ab_w_in, pool_w=pool_w, pool_scale=pool_scale,
             hy_conv_w=hy_conv_w, hy_conv_b=hy_conv_b, hy_ff_w1=hy_ff_w1, hy_ff_b1=hy_ff_b1,
             hy_ff_w2=hy_ff_w2, hy_ff_b2=hy_ff_b2, hy_ff_w3=hy_ff_w3, hy_ff_b3=hy_ff_b3,
             hy_ff_w4=hy_ff_w4, hy_ff_b4=hy_ff_b4, hy_freq=hy_freq, hy_d=hy_d, ab_w_out=ab_w_out,
             attn_w_qkv=attn_w_qkv, attn_w_o=attn_w_o, attn_sink=attn_sink, rel_bias=rel_bias,
             norm_ffn2=norm_ffn2, ffn2_wi=ffn2_wi, ffn2_wo=ffn2_wo, norm_final=norm_final)
    y_prompt = trunk(x_prompt, p)
    y_sample = trunk(x_sample, p)
    return (y_prompt, y_sample)
```

```python
import functools
import math

import jax
import jax.numpy as jnp
import numpy as np
from jax import lax
from jax.experimental import pallas as pl
from jax.experimental.pallas import tpu as pltpu

F32 = jnp.float32
BF16 = jnp.bfloat16
HIGHEST = lax.Precision.HIGHEST

D_MODEL = 2048
EPS = 1e-6
POOL_WIDTH = 1024
POOL_WINDOWS = (2, 4, 8, 16)
POOL_GROUP = 256
HYENA_WIDTH = 1024
HYENA_EMB_BANDS = 16
HYENA_FAST_DECAY = 0.3
HYENA_SLOW_DECAY = 1.5
HYENA_TARGET = 1e-2
N_HEADS = 16
N_KV_HEADS = 4
HEAD_DIM = 128
GQA_GROUP = 4
WINDOW = 128
BLOCK = 128
N_BUCKETS = 32
MAX_DISTANCE = 128
MASKED = -1e30
HALO = 16
MIB = 1024 * 1024


def _params(sem, vmem_mib):
    return pltpu.CompilerParams(dimension_semantics=sem, vmem_limit_bytes=vmem_mib * MIB)


def _seq_info(r0, lp, ls):
    in_prompt = r0 < lp
    idx = jnp.maximum(r0 - lp, 0) // ls
    start = jnp.where(in_prompt, 0, lp + idx * ls)
    length = jnp.where(in_prompt, lp, ls)
    return start, length


def _rmsnorm(x, g):
    ms = jnp.mean(x * x, axis=-1, keepdims=True)
    return x * lax.rsqrt(ms + EPS) * g


def _ffn_kernel(x_ref, g_ref, wg_ref, wu_ref, wo_ref, o_ref, h_ref):
    j = pl.program_id(1)

    @pl.when(j == 0)
    def _():
        h_ref[...] = _rmsnorm(x_ref[...], g_ref[...]).astype(BF16)

    h = h_ref[...]
    gate = jnp.dot(h, wg_ref[...], preferred_element_type=F32)
    up = jnp.dot(h, wu_ref[...], preferred_element_type=F32)
    act = (gate * jax.nn.sigmoid(gate) * up).astype(BF16)
    part = jnp.dot(act, wo_ref[...], preferred_element_type=F32)

    @pl.when(j == 0)
    def _():
        o_ref[...] = part

    @pl.when(j > 0)
    def _():
        o_ref[...] += part

    @pl.when(j == pl.num_programs(1) - 1)
    def _():
        o_ref[...] = x_ref[...] + 0.5 * o_ref[...]


def _ffn(x, g, wi, wo, layer, tm, tf):
    t, d = x.shape
    dff = wo.shape[1]
    nj = dff // tf
    return pl.pallas_call(
        _ffn_kernel,
        grid=(t // tm, nj),
        in_specs=[
            pl.BlockSpec((tm, d), lambda i, j: (i, 0)),
            pl.BlockSpec((1, d), lambda i, j: (0, 0)),
            pl.BlockSpec((None, d, tf), lambda i, j: (layer, 0, j)),
            pl.BlockSpec((None, d, tf), lambda i, j: (layer, 0, j + nj)),
            pl.BlockSpec((None, tf, d), lambda i, j: (layer, j, 0)),
        ],
        out_specs=pl.BlockSpec((tm, d), lambda i, j: (i, 0)),
        out_shape=jax.ShapeDtypeStruct((t, d), F32),
        scratch_shapes=[pltpu.VMEM((tm, d), BF16)],
        compiler_params=_params(("parallel", "arbitrary"), 56),
    )(x, g.reshape(1, d), wi, wi, wo)


def _norm_matmul_kernel(x_ref, g_ref, w_ref, o_ref, h_ref):
    @pl.when(pl.program_id(1) == 0)
    def _():
        h_ref[...] = _rmsnorm(x_ref[...], g_ref[...]).astype(BF16)

    o_ref[...] = jnp.dot(h_ref[...], w_ref[...], preferred_element_type=F32).astype(o_ref.dtype)


def _norm_matmul(x, g, w, layer, tm, tn):
    t, d = x.shape
    n = w.shape[2]
    return pl.pallas_call(
        _norm_matmul_kernel,
        grid=(t // tm, n // tn),
        in_specs=[
            pl.BlockSpec((tm, d), lambda i, j: (i, 0)),
            pl.BlockSpec((1, d), lambda i, j: (0, 0)),
            pl.BlockSpec((None, d, tn), lambda i, j: (layer, 0, j)),
        ],
        out_specs=pl.BlockSpec((tm, tn), lambda i, j: (i, j)),
        out_shape=jax.ShapeDtypeStruct((t, n), BF16),
        scratch_shapes=[pltpu.VMEM((tm, d), BF16)],
        compiler_params=_params(("parallel", "arbitrary"), 48),
    )(x, g.reshape(1, d), w)


def _proj_residual_kernel(*refs):
    x_ref, o_ref = refs[0], refs[-1]
    pairs = refs[1:-1]
    acc = x_ref[...]
    for k in range(len(pairs) // 2):
        acc = acc + jnp.dot(pairs[2 * k][...], pairs[2 * k + 1][...], preferred_element_type=F32)
    o_ref[...] = acc


def _proj_residual(x, ys, w, layer, tm):
    t, d = x.shape
    kw = ys[0].shape[1]
    in_specs = [pl.BlockSpec((tm, d), lambda i: (i, 0))]
    args = [x]
    for k, y in enumerate(ys):
        in_specs.append(pl.BlockSpec((tm, kw), lambda i: (i, 0)))
        in_specs.append(pl.BlockSpec((None, kw, d), lambda i, k=k: (layer, k, 0)))
        args += [y, w]
    return pl.pallas_call(
        _proj_residual_kernel,
        grid=(t // tm,),
        in_specs=in_specs,
        out_specs=pl.BlockSpec((tm, d), lambda i: (i, 0)),
        out_shape=jax.ShapeDtypeStruct((t, d), F32),
        compiler_params=_params(("parallel",), 48),
    )(*args)


def _final_norm_kernel(x_ref, g_ref, o_ref):
    o_ref[...] = _rmsnorm(x_ref[...], g_ref[...])


def _final_norm(x, g, row0, rows, tm):
    d = x.shape[1]
    off = row0 // tm
    return pl.pallas_call(
        _final_norm_kernel,
        grid=(rows // tm,),
        in_specs=[pl.BlockSpec((tm, d), lambda i: (i + off, 0)), pl.BlockSpec((1, d), lambda i: (0, 0))],
        out_specs=pl.BlockSpec((tm, d), lambda i: (i, 0)),
        out_shape=jax.ShapeDtypeStruct((rows, d), F32),
        compiler_params=_params(("parallel",), 32),
    )(x, g.reshape(1, d))


def _pool_kernel(um_ref, up_ref, un_ref, w_ref, sc_ref, o_ref, ext_ref, *, tt, lp, ls):
    r0 = pl.program_id(0) * tt
    start, length = _seq_info(r0, lp, ls)
    first = r0 == start
    last = r0 + tt == start + length
    ext_ref[0:HALO, :] = jnp.where(first, 0.0, up_ref[...].astype(F32))
    ext_ref[HALO:HALO + tt, :] = um_ref[...].astype(F32)
    ext_ref[HALO + tt:, :] = jnp.where(last, 0.0, un_ref[...].astype(F32))
    pos = (r0 - start) + lax.broadcasted_iota(jnp.int32, (tt, 1), 0)
    for g, w in enumerate(POOL_WINDOWS):
        cs = slice(g * POOL_GROUP, (g + 1) * POOL_GROUP)
        s = ext_ref[pl.ds(HALO - w // 2, tt), cs]
        for off in range(-w // 2 + 1, w // 2):
            s = s + ext_ref[pl.ds(HALO + off, tt), cs]
        cnt = (jnp.minimum(pos + w // 2, length) - jnp.maximum(pos - w // 2, 0)).astype(F32)
        p = s / cnt - ext_ref[pl.ds(HALO, tt), cs]
        y = jnp.dot(p.astype(BF16), w_ref[g], preferred_element_type=F32) * sc_ref[:, cs]
        o_ref[:, cs] = y.astype(BF16)


def _pool(u, pool_w, scale, tt, lp, ls):
    t = u.shape[0]
    nh = t // HALO
    hb = tt // HALO
    return pl.pallas_call(
        functools.partial(_pool_kernel, tt=tt, lp=lp, ls=ls),
        grid=(t // tt,),
        in_specs=[
            pl.BlockSpec((tt, POOL_WIDTH), lambda i: (i, 0)),
            pl.BlockSpec((HALO, POOL_WIDTH), lambda i: (jnp.maximum(i * hb - 1, 0), 0)),
            pl.BlockSpec((HALO, POOL_WIDTH), lambda i: (jnp.minimum((i + 1) * hb, nh - 1), 0)),
            pl.BlockSpec((len(POOL_WINDOWS), POOL_GROUP, POOL_GROUP), lambda i: (0, 0, 0)),
            pl.BlockSpec((1, POOL_WIDTH), lambda i: (0, 0)),
        ],
        out_specs=pl.BlockSpec((tt, POOL_WIDTH), lambda i: (i, 0)),
        out_shape=jax.ShapeDtypeStruct((t, POOL_WIDTH), BF16),
        scratch_shapes=[pltpu.VMEM((tt + 2 * HALO, POOL_WIDTH), F32)],
        compiler_params=_params(("parallel",), 32),
    )(u, u, u, pool_w, scale.reshape(1, POOL_WIDTH))


def _gate_kernel(m0, m1, m2, p0, p1, p2, n0, n1, n2, cw_ref, cb_ref,
                 vvb_ref, vvf_ref, x0_ref, nyq_ref, *, tt, lp, ls):
    r0 = pl.program_id(0) * tt
    start, length = _seq_info(r0, lp, ls)
    first = r0 == start
    last = r0 + tt == start + length
    row = lax.broadcasted_iota(jnp.int32, (tt, 1), 0)

    def conv(m_ref, p_ref, n_ref, q):
        cs = slice(q * HYENA_WIDTH, (q + 1) * HYENA_WIDTH)
        m = m_ref[...].astype(F32)
        before = jnp.where(first, 0.0, p_ref[...].astype(F32)[HALO - 1:HALO, :])
        after = jnp.where(last, 0.0, n_ref[...].astype(F32)[0:1, :])
        up = jnp.where(row == 0, before, pltpu.roll(m, 1, 0))
        dn = jnp.where(row == tt - 1, after, pltpu.roll(m, tt - 1, 0))
        return up * cw_ref[0:1, cs] + m * cw_ref[1:2, cs] + dn * cw_ref[2:3, cs] + cb_ref[:, cs]

    x0 = conv(m0, p0, n0, 0)
    x1 = conv(m1, p1, n1, 1)
    v = conv(m2, p2, n2, 2)
    vv = v * x1
    vvb_ref[...] = vv.astype(BF16)
    vvf_ref[...] = vv
    x0_ref[...] = x0
    sign = jnp.where((row & 1) == 0, 1.0, -1.0)
    part = (vv * sign).reshape(tt // 8, 8, HYENA_WIDTH).sum(axis=0)

    @pl.when(first)
    def _():
        nyq_ref[...] = part

    @pl.when(jnp.logical_not(first))
    def _():
        nyq_ref[...] += part


def _gate(u, conv_w, conv_b, tt, lp, ls, n_seq):
    t = u.shape[0]
    nh = t // HALO
    hb = tt // HALO
    c = HYENA_WIDTH

    def seq_of(i):
        r0 = i * tt
        return jnp.where(r0 < lp, 0, 1 + jnp.maximum(r0 - lp, 0) // ls)

    main = [pl.BlockSpec((tt, c), lambda i, q=q: (i, 1 + q)) for q in range(3)]
    prev = [pl.BlockSpec((HALO, c), lambda i, q=q: (jnp.maximum(i * hb - 1, 0), 1 + q)) for q in range(3)]
    nxt = [pl.BlockSpec((HALO, c), lambda i, q=q: (jnp.minimum((i + 1) * hb, nh - 1), 1 + q)) for q in range(3)]
    return pl.pallas_call(
        functools.partial(_gate_kernel, tt=tt, lp=lp, ls=ls),
        grid=(t // tt,),
        in_specs=main + prev + nxt + [
            pl.BlockSpec((3, 3 * c), lambda i: (0, 0)),
            pl.BlockSpec((1, 3 * c), lambda i: (0, 0)),
        ],
        out_specs=[
            pl.BlockSpec((tt, c), lambda i: (i, 0)),
            pl.BlockSpec((tt, c), lambda i: (i, 0)),
            pl.BlockSpec((tt, c), lambda i: (i, 0)),
            pl.BlockSpec((None, 8, c), lambda i: (seq_of(i), 0, 0)),
        ],
        out_shape=[
            jax.ShapeDtypeStruct((t, c), BF16),
            jax.ShapeDtypeStruct((t, c), F32),
            jax.ShapeDtypeStruct((t, c), F32),
            jax.ShapeDtypeStruct((n_seq, 8, c), F32),
        ],
        compiler_params=_params(("arbitrary",), 48),
    )(*([u] * 9), conv_w, conv_b.reshape(1, 3 * c))


def _filter_kernel(z_ref, w1, b1, w2, b2, w3, b3, w4, b4, fr_ref, dl_ref, e_ref, o_ref, st_ref, *, tr):
    i = pl.program_id(0)
    z = z_ref[...]
    fr = fr_ref[...]

    def dot(a, b):
        return jnp.dot(a, b, preferred_element_type=F32, precision=HIGHEST)

    h = jnp.sin(fr * (dot(z, w1[...]) + b1[...]))
    h = jnp.sin(fr * (dot(h, w2[...]) + b2[...]))
    h = jnp.sin(fr * (dot(h, w3[...]) + b3[...]))
    hh = dot(h, w4[...]) + b4[...]
    decay = jnp.exp(-z[:, 0:1] * jnp.abs(dl_ref[...]))
    row = i * tr + lax.broadcasted_iota(jnp.int32, (tr, 1), 0)
    hf = hh[:, :HYENA_WIDTH] * decay
    hb = jnp.where(row == 0, 0.0, hh[:, HYENA_WIDTH:] * decay)
    e = hf + hb
    e_ref[...] = e.astype(BF16)
    o_ref[...] = (hf - hb).astype(BF16)
    sign = jnp.where((row & 1) == 0, 1.0, -1.0)
    norm_part = (jnp.abs(hf) + jnp.abs(hb)).reshape(tr // 8, 8, HYENA_WIDTH).sum(axis=0)
    nyq_part = (e * sign).reshape(tr // 8, 8, HYENA_WIDTH).sum(axis=0)

    @pl.when(i == 0)
    def _():
        st_ref[0] = norm_part
        st_ref[1] = nyq_part

    @pl.when(i > 0)
    def _():
        st_ref[0] += norm_part
        st_ref[1] += nyq_part


def _hyena_filter(length, fp, tr):
    c = HYENA_WIDTH
    t = jnp.linspace(0.0, 1.0, length, dtype=F32)[:, None]
    w = 2.0 * math.pi * jnp.arange(length, dtype=F32)[:, None] / length
    f = jnp.linspace(1e-4, HYENA_EMB_BANDS - 1, HYENA_EMB_BANDS, dtype=F32)[None, :]
    z = jnp.concatenate([t, jnp.cos(f * w), -jnp.sin(f * w)], axis=-1)
    z = jnp.pad(z, ((0, 0), (0, 128 - z.shape[1])))
    min_decay = math.log(HYENA_TARGET) / HYENA_SLOW_DECAY
    max_decay = math.log(HYENA_TARGET) / HYENA_FAST_DECAY
    deltas = jnp.linspace(min_decay, max_decay, c, dtype=F32)[None, :]
    full = lambda shape: pl.BlockSpec(shape, lambda i: (0,) * len(shape))
    return pl.pallas_call(
        functools.partial(_filter_kernel, tr=tr),
        grid=(length // tr,),
        in_specs=[pl.BlockSpec((tr, 128), lambda i: (i, 0))] + [full(a.shape) for a in fp] + [full((1, c))],
        out_specs=[
            pl.BlockSpec((tr, c), lambda i: (i, 0)),
            pl.BlockSpec((tr, c), lambda i: (i, 0)),
            full((2, 8, c)),
        ],
        out_shape=[
            jax.ShapeDtypeStruct((length, c), BF16),
            jax.ShapeDtypeStruct((length, c), BF16),
            jax.ShapeDtypeStruct((2, 8, c), F32),
        ],
        compiler_params=_params(("arbitrary",), 48),
    )(z, *fp, deltas)


def _pad_filter_params(w1, b1, w2, b2, w3, b3, w4, b4, freq):
    hid = 128
    pad2 = lambda a, r, c: jnp.pad(a.astype(F32), ((0, r - a.shape[0]), (0, c - a.shape[1])))
    row = lambda a: jnp.pad(a.astype(F32), (0, hid - a.shape[0])).reshape(1, hid)
    return (pad2(w1, 128, hid), row(b1), pad2(w2, hid, hid), row(b2), pad2(w3, hid, hid), row(b3),
            pad2(w4, hid, w4.shape[1]), b4.astype(F32).reshape(1, -1), row(freq))


def _dft_mats(length):
    k = jnp.arange(length, dtype=jnp.int32)
    m = (k[:, None] * k[None, :]) % (2 * length)
    ang = m.astype(F32) * (2.0 * math.pi / (2 * length))
    return jnp.cos(ang).astype(BF16), jnp.sin(ang).astype(BF16)


def _spectrum_scale(st_ref, k0, tm, length):
    norm = jnp.sum(st_ref[0], axis=0, keepdims=True)
    krow = k0 + lax.broadcasted_iota(jnp.int32, (tm, 1), 0)
    wk = jnp.where(krow == 0, 1.0, 2.0)
    return wk / (2.0 * length * norm)


def _dft_filter_kernel(c_ref, s_ref, e_ref, o_ref, st_ref, kr_ref, ki_ref, accr, acci, *, tm, length):
    kt = pl.program_id(1)

    @pl.when(kt == 0)
    def _():
        accr[...] = jnp.zeros_like(accr)
        acci[...] = jnp.zeros_like(acci)

    accr[...] += jnp.dot(c_ref[...], e_ref[...], preferred_element_type=F32)
    acci[...] += jnp.dot(s_ref[...], o_ref[...], preferred_element_type=F32)

    @pl.when(kt == pl.num_programs(1) - 1)
    def _():
        sc = _spectrum_scale(st_ref, pl.program_id(0) * tm, tm, length)
        kr_ref[...] = accr[...] * sc
        ki_ref[...] = -acci[...] * sc


def _dft_filter(cm, sm, e, o, st, tm, tk):
    length = cm.shape[0]
    c = HYENA_WIDTH
    return pl.pallas_call(
        functools.partial(_dft_filter_kernel, tm=tm, length=length),
        grid=(length // tm, length // tk),
        in_specs=[
            pl.BlockSpec((tm, tk), lambda m, k: (m, k)),
            pl.BlockSpec((tm, tk), lambda m, k: (m, k)),
            pl.BlockSpec((tk, c), lambda m, k: (k, 0)),
            pl.BlockSpec((tk, c), lambda m, k: (k, 0)),
            pl.BlockSpec((2, 8, c), lambda m, k: (0, 0, 0)),
        ],
        out_specs=[pl.BlockSpec((tm, c), lambda m, k: (m, 0))] * 2,
        out_shape=[jax.ShapeDtypeStruct((length, c), F32)] * 2,
        scratch_shapes=[pltpu.VMEM((tm, c), F32)] * 2,
        compiler_params=_params(("parallel", "arbitrary"), 48),
    )(cm, sm, e, o, st)


def _dft_fwd_kernel(c_ref, s_ref, v_ref, kr_ref, ki_ref, yr_ref, yi_ref, accr, acci):
    kt = pl.program_id(2)

    @pl.when(kt == 0)
    def _():
        accr[...] = jnp.zeros_like(accr)
        acci[...] = jnp.zeros_like(acci)

    v = v_ref[...]
    accr[...] += jnp.dot(c_ref[...], v, preferred_element_type=F32)
    acci[...] += jnp.dot(s_ref[...], v, preferred_element_type=F32)

    @pl.when(kt == pl.num_programs(2) - 1)
    def _():
        vr = accr[...]
        vi = -acci[...]
        kr = kr_ref[...]
        ki = ki_ref[...]
        yr_ref[...] = (vr * kr - vi * ki).astype(BF16)
        yi_ref[...] = (vr * ki + vi * kr).astype(BF16)


def _dft_fwd(cm, sm, vvb, kr, ki, row0, batch, tm, tk):
    length = cm.shape[0]
    c = HYENA_WIDTH
    nm, nk = length // tm, length // tk
    off_k = row0 // tk
    return pl.pallas_call(
        _dft_fwd_kernel,
        grid=(nm, batch, nk),
        in_specs=[
            pl.BlockSpec((tm, tk), lambda m, b, k: (m, k)),
            pl.BlockSpec((tm, tk), lambda m, b, k: (m, k)),
            pl.BlockSpec((tk, c), lambda m, b, k: (off_k + b * nk + k, 0)),
            pl.BlockSpec((tm, c), lambda m, b, k: (m, 0)),
            pl.BlockSpec((tm, c), lambda m, b, k: (m, 0)),
        ],
        out_specs=[pl.BlockSpec((tm, c), lambda m, b, k: (b * nm + m, 0))] * 2,
        out_shape=[jax.ShapeDtypeStruct((batch * length, c), BF16)] * 2,
        scratch_shapes=[pltpu.VMEM((tm, c), F32)] * 2,
        compiler_params=_params(("parallel", "parallel", "arbitrary"), 48),
    )(cm, sm, vvb, kr, ki)


def _dft_inv_kernel(c_ref, s_ref, yr_ref, yi_ref, st_ref, vn_ref, vv_ref, x0_ref, d_ref, o_ref, acc,
                    *, tm, length):
    kt = pl.program_id(2)

    @pl.when(kt == 0)
    def _():
        acc[...] = jnp.zeros_like(acc)

    acc[...] += (jnp.dot(c_ref[...], yr_ref[...], preferred_element_type=F32)
                 - jnp.dot(s_ref[...], yi_ref[...], preferred_element_type=F32))

    @pl.when(kt == pl.num_programs(2) - 1)
    def _():
        norm = jnp.sum(st_ref[0], axis=0, keepdims=True)
        knyq = jnp.sum(st_ref[1], axis=0, keepdims=True)
        vnyq = jnp.sum(vn_ref[...], axis=0, keepdims=True)
        ynyq = vnyq * knyq / (2.0 * length * norm)
        row = lax.broadcasted_iota(jnp.int32, (tm, 1), 0)
        sign = jnp.where((row & 1) == 0, 1.0, -1.0)
        y = acc[...] + sign * ynyq
        o_ref[...] = ((y + vv_ref[...] * d_ref[...]) * x0_ref[...]).astype(BF16)


def _dft_inv(cm, sm, yr, yi, st, vnyq, vvf, x0, d, row0, batch, tm, tk):
    length = cm.shape[0]
    c = HYENA_WIDTH
    nm, nk = length // tm, length // tk
    off_m = row0 // tm
    seq0 = 0 if row0 == 0 else 1
    return pl.pallas_call(
        functools.partial(_dft_inv_kernel, tm=tm, length=length),
        grid=(nm, batch, nk),
        in_specs=[
            pl.BlockSpec((tm, tk), lambda m, b, k: (m, k)),
            pl.BlockSpec((tm, tk), lambda m, b, k: (m, k)),
            pl.BlockSpec((tk, c), lambda m, b, k: (b * nk + k, 0)),
            pl.BlockSpec((tk, c), lambda m, b, k: (b * nk + k, 0)),
            pl.BlockSpec((2, 8, c), lambda m, b, k: (0, 0, 0)),
            pl.BlockSpec((None, 8, c), lambda m, b, k: (seq0 + b, 0, 0)),
            pl.BlockSpec((tm, c), lambda m, b, k: (off_m + b * nm + m, 0)),
            pl.BlockSpec((tm, c), lambda m, b, k: (off_m + b * nm + m, 0)),
            pl.BlockSpec((1, c), lambda m, b, k: (0, 0)),
        ],
        out_specs=pl.BlockSpec((tm, c), lambda m, b, k: (b * nm + m, 0)),
        out_shape=jax.ShapeDtypeStruct((batch * length, c), BF16),
        scratch_shapes=[pltpu.VMEM((tm, c), F32)],
        compiler_params=_params(("parallel", "parallel", "arbitrary"), 48),
    )(cm, sm, yr, yi, st, vnyq, vvf, x0, d.reshape(1, c))


def _t5_buckets(rel):
    nb = N_BUCKETS // 2
    max_exact = nb // 2
    ret = (rel > 0).astype(jnp.int32) * nb
    n = jnp.abs(rel)
    large = max_exact + (jnp.log(jnp.maximum(n, 1).astype(F32) / max_exact)
                         / math.log(MAX_DISTANCE / max_exact) * (nb - max_exact)).astype(jnp.int32)
    large = jnp.minimum(large, nb - 1)
    return ret + jnp.where(n < max_exact, n, large)


def _bias_kernel(bk_ref, rb_ref, o_ref):
    h = pl.program_id(0)
    bk = bk_ref[...]
    qi = lax.broadcasted_iota(jnp.int32, bk.shape, 0)
    pj = lax.broadcasted_iota(jnp.int32, bk.shape, 1) - BLOCK
    acc = jnp.zeros(bk.shape, F32)
    for b in range(N_BUCKETS):
        acc = jnp.where(bk == b, rb_ref[b, h], acc)
    o_ref[...] = jnp.where(jnp.abs(pj - qi) <= WINDOW, acc, MASKED)


def _attn_bias(rel_bias):
    qi = jnp.arange(BLOCK)[:, None]
    pj = jnp.arange(3 * BLOCK)[None, :] - BLOCK
    buckets = _t5_buckets(pj - qi).astype(jnp.int32)
    return pl.pallas_call(
        _bias_kernel,
        grid=(N_HEADS,),
        in_specs=[
            pl.BlockSpec((BLOCK, 3 * BLOCK), lambda h: (0, 0)),
            pl.BlockSpec(memory_space=pltpu.SMEM),
        ],
        out_specs=pl.BlockSpec((None, BLOCK, 3 * BLOCK), lambda h: (h, 0, 0)),
        out_shape=jax.ShapeDtypeStruct((N_HEADS, BLOCK, 3 * BLOCK), F32),
        compiler_params=_params(("arbitrary",), 16),
    )(buckets, rel_bias.astype(F32))


def _attn_kernel(q_ref, kp_ref, kc_ref, kn_ref, vp_ref, vc_ref, vn_ref, bias_ref, sink_ref, o_ref, *, lp, ls):
    r0 = pl.program_id(0) * BLOCK
    start, length = _seq_info(r0, lp, ls)
    first = r0 == start
    last = r0 + BLOCK == start + length
    rows = GQA_GROUP * BLOCK
    col = lax.broadcasted_iota(jnp.int32, (rows, 3 * BLOCK), 1)
    outside = (first & (col < BLOCK)) | (last & (col >= 2 * BLOCK))
    grp = lax.broadcasted_iota(jnp.int32, (rows, 1), 0) // BLOCK
    scale = 1.0 / math.sqrt(HEAD_DIM)
    for kh in range(N_KV_HEADS):
        hs = slice(kh * HEAD_DIM, (kh + 1) * HEAD_DIM)
        kcat = jnp.concatenate([kp_ref[:, hs], kc_ref[:, hs], kn_ref[:, hs]], axis=0)
        vcat = jnp.concatenate([vp_ref[:, hs], vc_ref[:, hs], vn_ref[:, hs]], axis=0)
        heads = [kh * GQA_GROUP + g for g in range(GQA_GROUP)]
        q4 = jnp.concatenate([q_ref[:, h * HEAD_DIM:(h + 1) * HEAD_DIM] for h in heads], axis=0)
        s = lax.dot_general(q4, kcat, (((1,), (1,)), ((), ())), preferred_element_type=F32) * scale
        s = jnp.where(outside, MASKED, s + bias_ref[kh])
        sink = jnp.zeros((rows, 1), F32)
        for g, h in enumerate(heads):
            sink = jnp.where(grp == g, sink_ref[h], sink)
        m = jnp.maximum(jnp.max(s, axis=-1, keepdims=True), sink)
        p = jnp.exp(s - m)
        denom = jnp.sum(p, axis=-1, keepdims=True) + jnp.exp(sink - m)
        o = jnp.dot(p.astype(BF16), vcat, preferred_element_type=F32) / denom
        for g, h in enumerate(heads):
            o_ref[:, h * HEAD_DIM:(h + 1) * HEAD_DIM] = o[g * BLOCK:(g + 1) * BLOCK].astype(BF16)


def _attention(qkv, bias, sink, lp, ls):
    t = qkv.shape[0]
    nb = t // BLOCK
    qw = N_HEADS * HEAD_DIM
    kvw = N_KV_HEADS * HEAD_DIM
    kcol = qw // kvw
    prev = lambda i: jnp.maximum(i - 1, 0)
    nxt = lambda i: jnp.minimum(i + 1, nb - 1)
    kv_specs = [pl.BlockSpec((BLOCK, kvw), lambda i, f=f, cb=cb: (f(i), cb))
                for cb in (kcol, kcol + 1) for f in (prev, lambda i: i, nxt)]
    return pl.pallas_call(
        functools.partial(_attn_kernel, lp=lp, ls=ls),
        grid=(nb,),
        in_specs=[pl.BlockSpec((BLOCK, qw), lambda i: (i, 0))] + kv_specs + [
            pl.BlockSpec((N_KV_HEADS, GQA_GROUP * BLOCK, 3 * BLOCK), lambda i: (0, 0, 0)),
            pl.BlockSpec(memory_space=pltpu.SMEM),
        ],
        out_specs=pl.BlockSpec((BLOCK, qw), lambda i: (i, 0)),
        out_shape=jax.ShapeDtypeStruct((t, qw), BF16),
        compiler_params=_params(("parallel",), 32),
    )(qkv, *([qkv] * 6), bias.reshape(N_KV_HEADS, GQA_GROUP * BLOCK, 3 * BLOCK), sink.astype(F32))


def _tiles(lp, ls, dff):
    tok = min(512, ls)
    return dict(
        tm=tok,
        tf=512 if dff % 512 == 0 else dff,
        tn=min(1024, ls),
        tt=min(512, ls),
        dm=min(512, ls),
        dk=min(2048, ls),
    )


def _forward(x_prompt, x_sample, p):
    depth = p['norm_ffn1'].shape[0]
    lp = x_prompt.shape[0] * x_prompt.shape[1]
    assert x_prompt.shape[0] == 1
    bs, ls = x_sample.shape[0], x_sample.shape[1]
    n_seq = 1 + bs
    assert lp % ls == 0
    dff = p['ffn1_wo'].shape[1]
    tl = _tiles(lp, ls, dff)
    tm, tt = tl['tm'], tl['tt']

    x = jnp.concatenate([x_prompt.reshape(lp, D_MODEL), x_sample.reshape(bs * ls, D_MODEL)], axis=0)
    bf = lambda a: a.astype(BF16)
    ffn1_wi, ffn1_wo = bf(p['ffn1_wi']), bf(p['ffn1_wo'])
    ffn2_wi, ffn2_wo = bf(p['ffn2_wi']), bf(p['ffn2_wo'])
    ab_w_in, ab_w_out, pool_w = bf(p['ab_w_in']), bf(p['ab_w_out']), bf(p['pool_w'])
    w_qkv, w_o = bf(p['attn_w_qkv']), bf(p['attn_w_o'])
    bias = _attn_bias(p['rel_bias'])
    segments = ((0, 1, lp), (lp, bs, ls))
    mats = {length: _dft_mats(length) for length in {lp, ls}}

    for layer in range(depth):
        x = _ffn(x, p['norm_ffn1'][layer], ffn1_wi, ffn1_wo, layer, tm, tl['tf'])
        i = layer // 2
        if layer % 2 == 0:
            u = _norm_matmul(x, p['norm_mix'][layer], ab_w_in, i, tm, tl['tn'])
            ya = _pool(u, pool_w[i], p['pool_scale'][i], tt, lp, ls)
            vvb, vvf, x0, vnyq = _gate(u, p['hy_conv_w'][i].astype(F32), p['hy_conv_b'][i].astype(F32),
                                       tt, lp, ls, n_seq)
            fp = _pad_filter_params(p['hy_ff_w1'][i], p['hy_ff_b1'][i], p['hy_ff_w2'][i], p['hy_ff_b2'][i],
                                    p['hy_ff_w3'][i], p['hy_ff_b3'][i], p['hy_ff_w4'][i], p['hy_ff_b4'][i],
                                    p['hy_freq'][i])
            d = p['hy_d'][i].astype(F32)
            parts = []
            for row0, batch, length in segments:
                cm, sm = mats[length]
                dm, dk = min(tl['dm'], length), min(tl['dk'], length)
                e, o, st = _hyena_filter(length, fp, min(512, length))
                kr, ki = _dft_filter(cm, sm, e, o, st, dm, dk)
                yr, yi = _dft_fwd(cm, sm, vvb, kr, ki, row0, batch, dm, dk)
                parts.append(_dft_inv(cm, sm, yr, yi, st, vnyq, vvf, x0, d, row0, batch, dm, dk))
            yb = jnp.concatenate(parts, axis=0)
            x = _proj_residual(x, [ya, yb], ab_w_out, i, tm)
        else:
            qkv = _norm_matmul(x, p['norm_mix'][layer], w_qkv, i, tm, tl['tn'])
            o = _attention(qkv, bias, p['attn_sink'][i], lp, ls)
            x = _proj_residual(x, [o], w_o, i, tm)
        x = _ffn(x, p['norm_ffn2'][layer], ffn2_wi, ffn2_wo, layer, tm, tl['tf'])

    y_prompt = _final_norm(x, p['norm_final'], 0, lp, tm).reshape(x_prompt.shape)
    y_sample = _final_norm(x, p['norm_final'], lp, bs * ls, tm).reshape(x_sample.shape)
    return y_prompt, y_sample


def kernel(x_prompt, x_sample, norm_ffn1, ffn1_wi, ffn1_wo, norm_mix, ab_w_in, pool_w, pool_scale, hy_conv_w, hy_conv_b, hy_ff_w1, hy_ff_b1, hy_ff_w2, hy_ff_b2, hy_ff_w3, hy_ff_b3, hy_ff_w4, hy_ff_b4, hy_freq, hy_d, ab_w_out, attn_w_qkv, attn_w_o, attn_sink, rel_bias, norm_ffn2, ffn2_wi, ffn2_wo, norm_final):
    p = dict(norm_ffn1=norm_ffn1, ffn1_wi=ffn1_wi, ffn1_wo=ffn1_wo, norm_mix=norm_mix,
             ab_w_in=ab_w_in, pool_w=pool_w, pool_scale=pool_scale,
             hy_conv_w=hy_conv_w, hy_conv_b=hy_conv_b, hy_ff_w1=hy_ff_w1, hy_ff_b1=hy_ff_b1,
             hy_ff_w2=hy_ff_w2, hy_ff_b2=hy_ff_b2, hy_ff_w3=hy_ff_w3, hy_ff_b3=hy_ff_b3,
             hy_ff_w4=hy_ff_w4, hy_ff_b4=hy_ff_b4, hy_freq=hy_freq, hy_d=hy_d, ab_w_out=ab_w_out,
             attn_w_qkv=attn_w_qkv, attn_w_o=attn_w_o, attn_sink=attn_sink, rel_bias=rel_bias,
             norm_ffn2=norm_ffn2, ffn2_wi=ffn2_wi, ffn2_wo=ffn2_wo, norm_final=norm_final)
    return _forward(x_prompt, x_sample, p)
```

```python
import functools
import math

import jax
import jax.numpy as jnp
import numpy as np
from jax import lax
from jax.experimental import pallas as pl
from jax.experimental.pallas import tpu as pltpu

F32 = jnp.float32
BF16 = jnp.bfloat16
HIGHEST = lax.Precision.HIGHEST

D_MODEL = 2048
EPS = 1e-6
POOL_WIDTH = 1024
POOL_WINDOWS = (2, 4, 8, 16)
POOL_GROUP = 256
HYENA_WIDTH = 1024
HYENA_EMB_BANDS = 16
HYENA_FAST_DECAY = 0.3
HYENA_SLOW_DECAY = 1.5
HYENA_TARGET = 1e-2
N_HEADS = 16
N_KV_HEADS = 4
HEAD_DIM = 128
GQA_GROUP = 4
WINDOW = 128
BLOCK = 128
N_BUCKETS = 32
MAX_DISTANCE = 128
MASKED = -1e30
HALO = 16
MIB = 1024 * 1024


def _params(sem, vmem_mib):
    return pltpu.CompilerParams(dimension_semantics=sem, vmem_limit_bytes=vmem_mib * MIB)


def _seq_info(r0, lp, ls):
    in_prompt = r0 < lp
    idx = jnp.maximum(r0 - lp, 0) // ls
    start = jnp.where(in_prompt, 0, lp + idx * ls)
    length = jnp.where(in_prompt, lp, ls)
    return start, length


def _rmsnorm(x, g):
    ms = jnp.mean(x * x, axis=-1, keepdims=True)
    return x * lax.rsqrt(ms + EPS) * g


def _ffn_kernel(x_ref, g_ref, wg_ref, wu_ref, wo_ref, o_ref, h_ref):
    j = pl.program_id(1)

    @pl.when(j == 0)
    def _():
        x = x_ref[...]
        h_ref[...] = _rmsnorm(x, g_ref[...]).astype(BF16)
        o_ref[...] = x

    h = h_ref[...]
    gate = jnp.dot(h, wg_ref[...], preferred_element_type=F32)
    up = jnp.dot(h, wu_ref[...], preferred_element_type=F32)
    act = (gate * jax.nn.sigmoid(gate) * (0.5 * up)).astype(BF16)
    o_ref[...] += jnp.dot(act, wo_ref[...], preferred_element_type=F32)


def _ffn(x, g, wi, wo, layer, tm, tf):
    t, d = x.shape
    dff = wo.shape[1]
    nj = dff // tf
    return pl.pallas_call(
        _ffn_kernel,
        grid=(t // tm, nj),
        in_specs=[
            pl.BlockSpec((tm, d), lambda i, j: (i, 0)),
            pl.BlockSpec((1, d), lambda i, j: (0, 0)),
            pl.BlockSpec((None, d, tf), lambda i, j: (layer, 0, j)),
            pl.BlockSpec((None, d, tf), lambda i, j: (layer, 0, j + nj)),
            pl.BlockSpec((None, tf, d), lambda i, j: (layer, j, 0)),
        ],
        out_specs=pl.BlockSpec((tm, d), lambda i, j: (i, 0)),
        out_shape=jax.ShapeDtypeStruct((t, d), F32),
        scratch_shapes=[pltpu.VMEM((tm, d), BF16)],
        compiler_params=_params(("parallel", "arbitrary"), 56),
    )(x, g.reshape(1, d), wi, wi, wo)


def _norm_matmul_kernel(x_ref, g_ref, w_ref, o_ref, h_ref):
    @pl.when(pl.program_id(1) == 0)
    def _():
        h_ref[...] = _rmsnorm(x_ref[...], g_ref[...]).astype(BF16)

    o_ref[...] = jnp.dot(h_ref[...], w_ref[...], preferred_element_type=F32).astype(o_ref.dtype)


def _norm_matmul(x, g, w, layer, tm, tn):
    t, d = x.shape
    n = w.shape[2]
    return pl.pallas_call(
        _norm_matmul_kernel,
        grid=(t // tm, n // tn),
        in_specs=[
            pl.BlockSpec((tm, d), lambda i, j: (i, 0)),
            pl.BlockSpec((1, d), lambda i, j: (0, 0)),
            pl.BlockSpec((None, d, tn), lambda i, j: (layer, 0, j)),
        ],
        out_specs=pl.BlockSpec((tm, tn), lambda i, j: (i, j)),
        out_shape=jax.ShapeDtypeStruct((t, n), BF16),
        scratch_shapes=[pltpu.VMEM((tm, d), BF16)],
        compiler_params=_params(("parallel", "arbitrary"), 48),
    )(x, g.reshape(1, d), w)


def _proj_residual_kernel(*refs):
    x_ref, o_ref = refs[0], refs[-1]
    pairs = refs[1:-1]
    acc = x_ref[...]
    for k in range(len(pairs) // 2):
        acc = acc + jnp.dot(pairs[2 * k][...], pairs[2 * k + 1][...], preferred_element_type=F32)
    o_ref[...] = acc


def _proj_residual(x, ys, w, layer, tm):
    t, d = x.shape
    kw = ys[0].shape[1]
    in_specs = [pl.BlockSpec((tm, d), lambda i: (i, 0))]
    args = [x]
    for k, y in enumerate(ys):
        in_specs.append(pl.BlockSpec((tm, kw), lambda i: (i, 0)))
        in_specs.append(pl.BlockSpec((None, kw, d), lambda i, k=k: (layer, k, 0)))
        args += [y, w]
    return pl.pallas_call(
        _proj_residual_kernel,
        grid=(t // tm,),
        in_specs=in_specs,
        out_specs=pl.BlockSpec((tm, d), lambda i: (i, 0)),
        out_shape=jax.ShapeDtypeStruct((t, d), F32),
        compiler_params=_params(("parallel",), 48),
    )(*args)


def _final_norm_kernel(x_ref, g_ref, o_ref):
    o_ref[...] = _rmsnorm(x_ref[...], g_ref[...])


def _final_norm(x, g, row0, rows, tm):
    d = x.shape[1]
    off = row0 // tm
    return pl.pallas_call(
        _final_norm_kernel,
        grid=(rows // tm,),
        in_specs=[pl.BlockSpec((tm, d), lambda i: (i + off, 0)), pl.BlockSpec((1, d), lambda i: (0, 0))],
        out_specs=pl.BlockSpec((tm, d), lambda i: (i, 0)),
        out_shape=jax.ShapeDtypeStruct((rows, d), F32),
        compiler_params=_params(("parallel",), 32),
    )(x, g.reshape(1, d))


def _pool_kernel(um_ref, up_ref, un_ref, w_ref, sc_ref, o_ref, ext_ref, *, tt, lp, ls):
    r0 = pl.program_id(0) * tt
    start, length = _seq_info(r0, lp, ls)
    first = r0 == start
    last = r0 + tt == start + length
    ext_ref[0:HALO, :] = jnp.where(first, 0.0, up_ref[...].astype(F32))
    ext_ref[HALO:HALO + tt, :] = um_ref[...].astype(F32)
    ext_ref[HALO + tt:, :] = jnp.where(last, 0.0, un_ref[...].astype(F32))
    pos = (r0 - start) + lax.broadcasted_iota(jnp.int32, (tt, 1), 0)
    for g, w in enumerate(POOL_WINDOWS):
        cs = slice(g * POOL_GROUP, (g + 1) * POOL_GROUP)
        s = ext_ref[pl.ds(HALO - w // 2, tt), cs]
        for off in range(-w // 2 + 1, w // 2):
            s = s + ext_ref[pl.ds(HALO + off, tt), cs]
        cnt = (jnp.minimum(pos + w // 2, length) - jnp.maximum(pos - w // 2, 0)).astype(F32)
        p = s / cnt - ext_ref[pl.ds(HALO, tt), cs]
        y = jnp.dot(p.astype(BF16), w_ref[g], preferred_element_type=F32) * sc_ref[:, cs]
        o_ref[:, cs] = y.astype(BF16)


def _pool(u, pool_w, scale, tt, lp, ls):
    t = u.shape[0]
    nh = t // HALO
    hb = tt // HALO
    return pl.pallas_call(
        functools.partial(_pool_kernel, tt=tt, lp=lp, ls=ls),
        grid=(t // tt,),
        in_specs=[
            pl.BlockSpec((tt, POOL_WIDTH), lambda i: (i, 0)),
            pl.BlockSpec((HALO, POOL_WIDTH), lambda i: (jnp.maximum(i * hb - 1, 0), 0)),
            pl.BlockSpec((HALO, POOL_WIDTH), lambda i: (jnp.minimum((i + 1) * hb, nh - 1), 0)),
            pl.BlockSpec((len(POOL_WINDOWS), POOL_GROUP, POOL_GROUP), lambda i: (0, 0, 0)),
            pl.BlockSpec((1, POOL_WIDTH), lambda i: (0, 0)),
        ],
        out_specs=pl.BlockSpec((tt, POOL_WIDTH), lambda i: (i, 0)),
        out_shape=jax.ShapeDtypeStruct((t, POOL_WIDTH), BF16),
        scratch_shapes=[pltpu.VMEM((tt + 2 * HALO, POOL_WIDTH), F32)],
        compiler_params=_params(("parallel",), 32),
    )(u, u, u, pool_w, scale.reshape(1, POOL_WIDTH))


def _gate_kernel(m0, m1, m2, p0, p1, p2, n0, n1, n2, cw_ref, cb_ref,
                 vvb_ref, vvf_ref, x0_ref, nyq_ref, *, tt, lp, ls):
    r0 = pl.program_id(0) * tt
    start, length = _seq_info(r0, lp, ls)
    first = r0 == start
    last = r0 + tt == start + length
    row = lax.broadcasted_iota(jnp.int32, (tt, 1), 0)

    def conv(m_ref, p_ref, n_ref, q):
        cs = slice(q * HYENA_WIDTH, (q + 1) * HYENA_WIDTH)
        m = m_ref[...].astype(F32)
        before = jnp.where(first, 0.0, p_ref[...].astype(F32)[HALO - 1:HALO, :])
        after = jnp.where(last, 0.0, n_ref[...].astype(F32)[0:1, :])
        up = jnp.where(row == 0, before, pltpu.roll(m, 1, 0))
        dn = jnp.where(row == tt - 1, after, pltpu.roll(m, tt - 1, 0))
        return up * cw_ref[0:1, cs] + m * cw_ref[1:2, cs] + dn * cw_ref[2:3, cs] + cb_ref[:, cs]

    x0 = conv(m0, p0, n0, 0)
    x1 = conv(m1, p1, n1, 1)
    v = conv(m2, p2, n2, 2)
    vv = v * x1
    vvb_ref[...] = vv.astype(BF16)
    vvf_ref[...] = vv
    x0_ref[...] = x0
    sign = jnp.where((row & 1) == 0, 1.0, -1.0)
    part = (vv * sign).reshape(tt // 8, 8, HYENA_WIDTH).sum(axis=0)

    @pl.when(first)
    def _():
        nyq_ref[...] = part

    @pl.when(jnp.logical_not(first))
    def _():
        nyq_ref[...] += part


def _gate(u, conv_w, conv_b, tt, lp, ls, n_seq):
    t = u.shape[0]
    nh = t // HALO
    hb = tt // HALO
    c = HYENA_WIDTH

    def seq_of(i):
        r0 = i * tt
        return jnp.where(r0 < lp, 0, 1 + jnp.maximum(r0 - lp, 0) // ls)

    main = [pl.BlockSpec((tt, c), lambda i, q=q: (i, 1 + q)) for q in range(3)]
    prev = [pl.BlockSpec((HALO, c), lambda i, q=q: (jnp.maximum(i * hb - 1, 0), 1 + q)) for q in range(3)]
    nxt = [pl.BlockSpec((HALO, c), lambda i, q=q: (jnp.minimum((i + 1) * hb, nh - 1), 1 + q)) for q in range(3)]
    return pl.pallas_call(
        functools.partial(_gate_kernel, tt=tt, lp=lp, ls=ls),
        grid=(t // tt,),
        in_specs=main + prev + nxt + [
            pl.BlockSpec((3, 3 * c), lambda i: (0, 0)),
            pl.BlockSpec((1, 3 * c), lambda i: (0, 0)),
        ],
        out_specs=[
            pl.BlockSpec((tt, c), lambda i: (i, 0)),
            pl.BlockSpec((tt, c), lambda i: (i, 0)),
            pl.BlockSpec((tt, c), lambda i: (i, 0)),
            pl.BlockSpec((None, 8, c), lambda i: (seq_of(i), 0, 0)),
        ],
        out_shape=[
            jax.ShapeDtypeStruct((t, c), BF16),
            jax.ShapeDtypeStruct((t, c), F32),
            jax.ShapeDtypeStruct((t, c), F32),
            jax.ShapeDtypeStruct((n_seq, 8, c), F32),
        ],
        compiler_params=_params(("arbitrary",), 48),
    )(*([u] * 9), conv_w, conv_b.reshape(1, 3 * c))


def _filter_kernel(z_ref, w1, b1, w2, b2, w3, b3, w4, b4, fr_ref, dl_ref, e_ref, o_ref, st_ref, *, tr):
    i = pl.program_id(0)
    z = z_ref[...]
    fr = fr_ref[...]

    def dot(a, b):
        return jnp.dot(a, b, preferred_element_type=F32, precision=HIGHEST)

    h = jnp.sin(fr * (dot(z, w1[...]) + b1[...]))
    h = jnp.sin(fr * (dot(h, w2[...]) + b2[...]))
    h = jnp.sin(fr * (dot(h, w3[...]) + b3[...]))
    hh = dot(h, w4[...]) + b4[...]
    decay = jnp.exp(-z[:, 0:1] * jnp.abs(dl_ref[...]))
    row = i * tr + lax.broadcasted_iota(jnp.int32, (tr, 1), 0)
    hf = hh[:, :HYENA_WIDTH] * decay
    hb = jnp.where(row == 0, 0.0, hh[:, HYENA_WIDTH:] * decay)
    e = hf + hb
    e_ref[...] = e.astype(BF16)
    o_ref[...] = (hf - hb).astype(BF16)
    sign = jnp.where((row & 1) == 0, 1.0, -1.0)
    norm_part = (jnp.abs(hf) + jnp.abs(hb)).reshape(tr // 8, 8, HYENA_WIDTH).sum(axis=0)
    nyq_part = (e * sign).reshape(tr // 8, 8, HYENA_WIDTH).sum(axis=0)

    @pl.when(i == 0)
    def _():
        st_ref[0] = norm_part
        st_ref[1] = nyq_part

    @pl.when(i > 0)
    def _():
        st_ref[0] += norm_part
        st_ref[1] += nyq_part


def _hyena_filter(length, fp, tr):
    c = HYENA_WIDTH
    t = jnp.linspace(0.0, 1.0, length, dtype=F32)[:, None]
    w = 2.0 * math.pi * jnp.arange(length, dtype=F32)[:, None] / length
    f = jnp.linspace(1e-4, HYENA_EMB_BANDS - 1, HYENA_EMB_BANDS, dtype=F32)[None, :]
    z = jnp.concatenate([t, jnp.cos(f * w), -jnp.sin(f * w)], axis=-1)
    z = jnp.pad(z, ((0, 0), (0, 128 - z.shape[1])))
    min_decay = math.log(HYENA_TARGET) / HYENA_SLOW_DECAY
    max_decay = math.log(HYENA_TARGET) / HYENA_FAST_DECAY
    deltas = jnp.linspace(min_decay, max_decay, c, dtype=F32)[None, :]
    full = lambda shape: pl.BlockSpec(shape, lambda i: (0,) * len(shape))
    return pl.pallas_call(
        functools.partial(_filter_kernel, tr=tr),
        grid=(length // tr,),
        in_specs=[pl.BlockSpec((tr, 128), lambda i: (i, 0))] + [full(a.shape) for a in fp] + [full((1, c))],
        out_specs=[
            pl.BlockSpec((tr, c), lambda i: (i, 0)),
            pl.BlockSpec((tr, c), lambda i: (i, 0)),
            full((2, 8, c)),
        ],
        out_shape=[
            jax.ShapeDtypeStruct((length, c), BF16),
            jax.ShapeDtypeStruct((length, c), BF16),
            jax.ShapeDtypeStruct((2, 8, c), F32),
        ],
        compiler_params=_params(("arbitrary",), 48),
    )(z, *fp, deltas)


def _pad_filter_params(w1, b1, w2, b2, w3, b3, w4, b4, freq):
    hid = 128
    pad2 = lambda a, r, c: jnp.pad(a.astype(F32), ((0, r - a.shape[0]), (0, c - a.shape[1])))
    row = lambda a: jnp.pad(a.astype(F32), (0, hid - a.shape[0])).reshape(1, hid)
    return (pad2(w1, 128, hid), row(b1), pad2(w2, hid, hid), row(b2), pad2(w3, hid, hid), row(b3),
            pad2(w4, hid, w4.shape[1]), b4.astype(F32).reshape(1, -1), row(freq))


def _dft_mats(length):
    n = 2 * length
    k = jnp.arange(length, dtype=jnp.int32)[:, None]

    def table(cols):
        ang = ((k * cols[None, :]) % n).astype(F32) * (2.0 * math.pi / n)
        return jnp.cos(ang), jnp.sin(ang)

    ca, sa = table(jnp.arange(length // 128, dtype=jnp.int32) * 128)
    cb, sb = table(jnp.arange(128, dtype=jnp.int32))
    cos = ca[:, :, None] * cb[:, None, :] - sa[:, :, None] * sb[:, None, :]
    sin = sa[:, :, None] * cb[:, None, :] + ca[:, :, None] * sb[:, None, :]
    return cos.reshape(length, length).astype(BF16), sin.reshape(length, length).astype(BF16)


def _spectrum_scale(st_ref, k0, tm, length):
    norm = jnp.sum(st_ref[0], axis=0, keepdims=True)
    krow = k0 + lax.broadcasted_iota(jnp.int32, (tm, 1), 0)
    wk = jnp.where(krow == 0, 1.0, 2.0)
    return wk / (2.0 * length * norm)


def _dft_filter_kernel(c_ref, s_ref, e_ref, o_ref, st_ref, kr_ref, ki_ref, accr, acci, *, tm, length):
    kt = pl.program_id(1)

    @pl.when(kt == 0)
    def _():
        accr[...] = jnp.zeros_like(accr)
        acci[...] = jnp.zeros_like(acci)

    accr[...] += jnp.dot(c_ref[...], e_ref[...], preferred_element_type=F32)
    acci[...] += jnp.dot(s_ref[...], o_ref[...], preferred_element_type=F32)

    @pl.when(kt == pl.num_programs(1) - 1)
    def _():
        sc = _spectrum_scale(st_ref, pl.program_id(0) * tm, tm, length)
        kr_ref[...] = accr[...] * sc
        ki_ref[...] = -acci[...] * sc


def _dft_filter(cm, sm, e, o, st, tm, tk):
    length = cm.shape[0]
    c = HYENA_WIDTH
    return pl.pallas_call(
        functools.partial(_dft_filter_kernel, tm=tm, length=length),
        grid=(length // tm, length // tk),
        in_specs=[
            pl.BlockSpec((tm, tk), lambda m, k: (m, k)),
            pl.BlockSpec((tm, tk), lambda m, k: (m, k)),
            pl.BlockSpec((tk, c), lambda m, k: (k, 0)),
            pl.BlockSpec((tk, c), lambda m, k: (k, 0)),
            pl.BlockSpec((2, 8, c), lambda m, k: (0, 0, 0)),
        ],
        out_specs=[pl.BlockSpec((tm, c), lambda m, k: (m, 0))] * 2,
        out_shape=[jax.ShapeDtypeStruct((length, c), F32)] * 2,
        scratch_shapes=[pltpu.VMEM((tm, c), F32)] * 2,
        compiler_params=_params(("parallel", "arbitrary"), 48),
    )(cm, sm, e, o, st)


def _dft_fwd_kernel(c_ref, s_ref, v_ref, kr_ref, ki_ref, yr_ref, yi_ref, accr, acci):
    kt = pl.program_id(2)

    @pl.when(kt == 0)
    def _():
        accr[...] = jnp.zeros_like(accr)
        acci[...] = jnp.zeros_like(acci)

    v = v_ref[...]
    accr[...] += jnp.dot(c_ref[...], v, preferred_element_type=F32)
    acci[...] += jnp.dot(s_ref[...], v, preferred_element_type=F32)

    @pl.when(kt == pl.num_programs(2) - 1)
    def _():
        vr = accr[...]
        vi = -acci[...]
        kr = kr_ref[...]
        ki = ki_ref[...]
        yr_ref[...] = (vr * kr - vi * ki).astype(BF16)
        yi_ref[...] = (vr * ki + vi * kr).astype(BF16)


def _dft_fwd(cm, sm, vvb, kr, ki, row0, batch, tm, tk):
    length = cm.shape[0]
    c = HYENA_WIDTH
    nm, nk = length // tm, length // tk
    off_k = row0 // tk
    return pl.pallas_call(
        _dft_fwd_kernel,
        grid=(nm, batch, nk),
        in_specs=[
            pl.BlockSpec((tm, tk), lambda m, b, k: (m, k)),
            pl.BlockSpec((tm, tk), lambda m, b, k: (m, k)),
            pl.BlockSpec((tk, c), lambda m, b, k: (off_k + b * nk + k, 0)),
            pl.BlockSpec((tm, c), lambda m, b, k: (m, 0)),
            pl.BlockSpec((tm, c), lambda m, b, k: (m, 0)),
        ],
        out_specs=[pl.BlockSpec((tm, c), lambda m, b, k: (b * nm + m, 0))] * 2,
        out_shape=[jax.ShapeDtypeStruct((batch * length, c), BF16)] * 2,
        scratch_shapes=[pltpu.VMEM((tm, c), F32)] * 2,
        compiler_params=_params(("parallel", "parallel", "arbitrary"), 48),
    )(cm, sm, vvb, kr, ki)


def _dft_inv_kernel(c_ref, s_ref, yr_ref, yi_ref, st_ref, vn_ref, vv_ref, x0_ref, d_ref, o_ref, acc,
                    *, tm, length):
    kt = pl.program_id(2)

    @pl.when(kt == 0)
    def _():
        acc[...] = jnp.zeros_like(acc)

    acc[...] += (jnp.dot(c_ref[...], yr_ref[...], preferred_element_type=F32)
                 - jnp.dot(s_ref[...], yi_ref[...], preferred_element_type=F32))

    @pl.when(kt == pl.num_programs(2) - 1)
    def _():
        norm = jnp.sum(st_ref[0], axis=0, keepdims=True)
        knyq = jnp.sum(st_ref[1], axis=0, keepdims=True)
        vnyq = jnp.sum(vn_ref[...], axis=0, keepdims=True)
        ynyq = vnyq * knyq / (2.0 * length * norm)
        row = lax.broadcasted_iota(jnp.int32, (tm, 1), 0)
        sign = jnp.where((row & 1) == 0, 1.0, -1.0)
        y = acc[...] + sign * ynyq
        o_ref[...] = ((y + vv_ref[...] * d_ref[...]) * x0_ref[...]).astype(BF16)


def _dft_inv(cm, sm, yr, yi, st, vnyq, vvf, x0, d, row0, batch, tm, tk):
    length = cm.shape[0]
    c = HYENA_WIDTH
    nm, nk = length // tm, length // tk
    off_m = row0 // tm
    seq0 = 0 if row0 == 0 else 1
    return pl.pallas_call(
        functools.partial(_dft_inv_kernel, tm=tm, length=length),
        grid=(nm, batch, nk),
        in_specs=[
            pl.BlockSpec((tm, tk), lambda m, b, k: (m, k)),
            pl.BlockSpec((tm, tk), lambda m, b, k: (m, k)),
            pl.BlockSpec((tk, c), lambda m, b, k: (b * nk + k, 0)),
            pl.BlockSpec((tk, c), lambda m, b, k: (b * nk + k, 0)),
            pl.BlockSpec((2, 8, c), lambda m, b, k: (0, 0, 0)),
            pl.BlockSpec((None, 8, c), lambda m, b, k: (seq0 + b, 0, 0)),
            pl.BlockSpec((tm, c), lambda m, b, k: (off_m + b * nm + m, 0)),
            pl.BlockSpec((tm, c), lambda m, b, k: (off_m + b * nm + m, 0)),
            pl.BlockSpec((1, c), lambda m, b, k: (0, 0)),
        ],
        out_specs=pl.BlockSpec((tm, c), lambda m, b, k: (b * nm + m, 0)),
        out_shape=jax.ShapeDtypeStruct((batch * length, c), BF16),
        scratch_shapes=[pltpu.VMEM((tm, c), F32)],
        compiler_params=_params(("parallel", "parallel", "arbitrary"), 48),
    )(cm, sm, yr, yi, st, vnyq, vvf, x0, d.reshape(1, c))


def _t5_buckets(rel):
    nb = N_BUCKETS // 2
    max_exact = nb // 2
    ret = (rel > 0).astype(jnp.int32) * nb
    n = jnp.abs(rel)
    large = max_exact + (jnp.log(jnp.maximum(n, 1).astype(F32) / max_exact)
                         / math.log(MAX_DISTANCE / max_exact) * (nb - max_exact)).astype(jnp.int32)
    large = jnp.minimum(large, nb - 1)
    return ret + jnp.where(n < max_exact, n, large)


def _bias_kernel(bk_ref, rb_ref, o_ref):
    h = pl.program_id(0)
    bk = bk_ref[...]
    qi = lax.broadcasted_iota(jnp.int32, bk.shape, 0)
    pj = lax.broadcasted_iota(jnp.int32, bk.shape, 1) - BLOCK
    acc = jnp.zeros(bk.shape, F32)
    for b in range(N_BUCKETS):
        acc = jnp.where(bk == b, rb_ref[b, h], acc)
    o_ref[...] = jnp.where(jnp.abs(pj - qi) <= WINDOW, acc, MASKED)


def _attn_bias(rel_bias):
    qi = jnp.arange(BLOCK)[:, None]
    pj = jnp.arange(3 * BLOCK)[None, :] - BLOCK
    buckets = _t5_buckets(pj - qi).astype(jnp.int32)
    return pl.pallas_call(
        _bias_kernel,
        grid=(N_HEADS,),
        in_specs=[
            pl.BlockSpec((BLOCK, 3 * BLOCK), lambda h: (0, 0)),
            pl.BlockSpec(memory_space=pltpu.SMEM),
        ],
        out_specs=pl.BlockSpec((None, BLOCK, 3 * BLOCK), lambda h: (h, 0, 0)),
        out_shape=jax.ShapeDtypeStruct((N_HEADS, BLOCK, 3 * BLOCK), F32),
        compiler_params=_params(("arbitrary",), 16),
    )(buckets, rel_bias.astype(F32))


def _attn_kernel(q_ref, kp_ref, kc_ref, kn_ref, vp_ref, vc_ref, vn_ref, bias_ref, sink_ref, o_ref, *, lp, ls):
    r0 = pl.program_id(0) * BLOCK
    start, length = _seq_info(r0, lp, ls)
    first = r0 == start
    last = r0 + BLOCK == start + length
    rows = GQA_GROUP * BLOCK
    col = lax.broadcasted_iota(jnp.int32, (rows, 3 * BLOCK), 1)
    outside = (first & (col < BLOCK)) | (last & (col >= 2 * BLOCK))
    grp = lax.broadcasted_iota(jnp.int32, (rows, 1), 0) // BLOCK
    scale = 1.0 / math.sqrt(HEAD_DIM)
    for kh in range(N_KV_HEADS):
        hs = slice(kh * HEAD_DIM, (kh + 1) * HEAD_DIM)
        kcat = jnp.concatenate([kp_ref[:, hs], kc_ref[:, hs], kn_ref[:, hs]], axis=0)
        vcat = jnp.concatenate([vp_ref[:, hs], vc_ref[:, hs], vn_ref[:, hs]], axis=0)
        heads = [kh * GQA_GROUP + g for g in range(GQA_GROUP)]
        q4 = jnp.concatenate([q_ref[:, h * HEAD_DIM:(h + 1) * HEAD_DIM] for h in heads], axis=0)
        s = lax.dot_general(q4, kcat, (((1,), (1,)), ((), ())), preferred_element_type=F32) * scale
        s = jnp.where(outside, MASKED, s + bias_ref[kh])
        sink = jnp.zeros((rows, 1), F32)
        for g, h in enumerate(heads):
            sink = jnp.where(grp == g, sink_ref[h], sink)
        m = jnp.maximum(jnp.max(s, axis=-1, keepdims=True), sink)
        p = jnp.exp(s - m)
        denom = jnp.sum(p, axis=-1, keepdims=True) + jnp.exp(sink - m)
        o = jnp.dot(p.astype(BF16), vcat, preferred_element_type=F32) / denom
        for g, h in enumerate(heads):
            o_ref[:, h * HEAD_DIM:(h + 1) * HEAD_DIM] = o[g * BLOCK:(g + 1) * BLOCK].astype(BF16)


def _attention(qkv, bias, sink, lp, ls):
    t = qkv.shape[0]
    nb = t // BLOCK
    qw = N_HEADS * HEAD_DIM
    kvw = N_KV_HEADS * HEAD_DIM
    kcol = qw // kvw
    prev = lambda i: jnp.maximum(i - 1, 0)
    nxt = lambda i: jnp.minimum(i + 1, nb - 1)
    kv_specs = [pl.BlockSpec((BLOCK, kvw), lambda i, f=f, cb=cb: (f(i), cb))
                for cb in (kcol, kcol + 1) for f in (prev, lambda i: i, nxt)]
    return pl.pallas_call(
        functools.partial(_attn_kernel, lp=lp, ls=ls),
        grid=(nb,),
        in_specs=[pl.BlockSpec((BLOCK, qw), lambda i: (i, 0))] + kv_specs + [
            pl.BlockSpec((N_KV_HEADS, GQA_GROUP * BLOCK, 3 * BLOCK), lambda i: (0, 0, 0)),
            pl.BlockSpec(memory_space=pltpu.SMEM),
        ],
        out_specs=pl.BlockSpec((BLOCK, qw), lambda i: (i, 0)),
        out_shape=jax.ShapeDtypeStruct((t, qw), BF16),
        compiler_params=_params(("parallel",), 32),
    )(qkv, *([qkv] * 6), bias.reshape(N_KV_HEADS, GQA_GROUP * BLOCK, 3 * BLOCK), sink.astype(F32))


def _tiles(lp, ls, dff):
    tok = min(512, ls)
    return dict(
        tm=tok,
        fm=min(1024, ls),
        tf=512 if dff % 512 == 0 else dff,
        tn=min(1024, ls),
        tt=min(512, ls),
        dm=min(512, ls),
        dk=min(2048, ls),
    )


def _forward(x_prompt, x_sample, p):
    depth = p['norm_ffn1'].shape[0]
    lp = x_prompt.shape[0] * x_prompt.shape[1]
    assert x_prompt.shape[0] == 1
    bs, ls = x_sample.shape[0], x_sample.shape[1]
    n_seq = 1 + bs
    assert lp % ls == 0
    dff = p['ffn1_wo'].shape[1]
    tl = _tiles(lp, ls, dff)
    tm, tt = tl['tm'], tl['tt']

    x = jnp.concatenate([x_prompt.reshape(lp, D_MODEL), x_sample.reshape(bs * ls, D_MODEL)], axis=0)
    bf = lambda a: a.astype(BF16)
    ffn1_wi, ffn1_wo = bf(p['ffn1_wi']), bf(p['ffn1_wo'])
    ffn2_wi, ffn2_wo = bf(p['ffn2_wi']), bf(p['ffn2_wo'])
    ab_w_in, ab_w_out, pool_w = bf(p['ab_w_in']), bf(p['ab_w_out']), bf(p['pool_w'])
    w_qkv, w_o = bf(p['attn_w_qkv']), bf(p['attn_w_o'])
    bias = _attn_bias(p['rel_bias'])
    segments = ((0, 1, lp), (lp, bs, ls))
    mats = {length: _dft_mats(length) for length in {lp, ls}}

    for layer in range(depth):
        x = _ffn(x, p['norm_ffn1'][layer], ffn1_wi, ffn1_wo, layer, tl['fm'], tl['tf'])
        i = layer // 2
        if layer % 2 == 0:
            u = _norm_matmul(x, p['norm_mix'][layer], ab_w_in, i, tm, tl['tn'])
            ya = _pool(u, pool_w[i], p['pool_scale'][i], tt, lp, ls)
            vvb, vvf, x0, vnyq = _gate(u, p['hy_conv_w'][i].astype(F32), p['hy_conv_b'][i].astype(F32),
                                       tt, lp, ls, n_seq)
            fp = _pad_filter_params(p['hy_ff_w1'][i], p['hy_ff_b1'][i], p['hy_ff_w2'][i], p['hy_ff_b2'][i],
                                    p['hy_ff_w3'][i], p['hy_ff_b3'][i], p['hy_ff_w4'][i], p['hy_ff_b4'][i],
                                    p['hy_freq'][i])
            d = p['hy_d'][i].astype(F32)
            parts = []
            for row0, batch, length in segments:
                cm, sm = mats[length]
                dm, dk = min(tl['dm'], length), min(tl['dk'], length)
                e, o, st = _hyena_filter(length, fp, min(512, length))
                kr, ki = _dft_filter(cm, sm, e, o, st, dm, dk)
                yr, yi = _dft_fwd(cm, sm, vvb, kr, ki, row0, batch, dm, dk)
                parts.append(_dft_inv(cm, sm, yr, yi, st, vnyq, vvf, x0, d, row0, batch, dm, dk))
            yb = jnp.concatenate(parts, axis=0)
            x = _proj_residual(x, [ya, yb], ab_w_out, i, tm)
        else:
            qkv = _norm_matmul(x, p['norm_mix'][layer], w_qkv, i, tm, tl['tn'])
            o = _attention(qkv, bias, p['attn_sink'][i], lp, ls)
            x = _proj_residual(x, [o], w_o, i, tm)
        x = _ffn(x, p['norm_ffn2'][layer], ffn2_wi, ffn2_wo, layer, tl['fm'], tl['tf'])

    y_prompt = _final_norm(x, p['norm_final'], 0, lp, tm).reshape(x_prompt.shape)
    y_sample = _final_norm(x, p['norm_final'], lp, bs * ls, tm).reshape(x_sample.shape)
    return y_prompt, y_sample


def kernel(x_prompt, x_sample, norm_ffn1, ffn1_wi, ffn1_wo, norm_mix, ab_w_in, pool_w, pool_scale, hy_conv_w, hy_conv_b, hy_ff_w1, hy_ff_b1, hy_ff_w2, hy_ff_b2, hy_ff_w3, hy_ff_b3, hy_ff_w4, hy_ff_b4, hy_freq, hy_d, ab_w_out, attn_w_qkv, attn_w_o, attn_sink, rel_bias, norm_ffn2, ffn2_wi, ffn2_wo, norm_final):
    p = dict(norm_ffn1=norm_ffn1, ffn1_wi=ffn1_wi, ffn1_wo=ffn1_wo, norm_mix=norm_mix,
             ab_w_in=ab_w_in, pool_w=pool_w, pool_scale=pool_scale,
             hy_conv_w=hy_conv_w, hy_conv_b=hy_conv_b, hy_ff_w1=hy_ff_w1, hy_ff_b1=hy_ff_b1,
             hy_ff_w2=hy_ff_w2, hy_ff_b2=hy_ff_b2, hy_ff_w3=hy_ff_w3, hy_ff_b3=hy_ff_b3,
             hy_ff_w4=hy_ff_w4, hy_ff_b4=hy_ff_b4, hy_freq=hy_freq, hy_d=hy_d, ab_w_out=ab_w_out,
             attn_w_qkv=attn_w_qkv, attn_w_o=attn_w_o, attn_sink=attn_sink, rel_bias=rel_bias,
             norm_ffn2=norm_ffn2, ffn2_wi=ffn2_wi, ffn2_wo=ffn2_wo, norm_final=norm_final)
    return _forward(x_prompt, x_sample, p)
```

```python
import functools
import math

import jax
import jax.numpy as jnp
import numpy as np
from jax import lax
from jax.experimental import pallas as pl
from jax.experimental.pallas import tpu as pltpu

F32 = jnp.float32
BF16 = jnp.bfloat16
HIGHEST = lax.Precision.HIGHEST

D_MODEL = 2048
EPS = 1e-6
POOL_WIDTH = 1024
POOL_WINDOWS = (2, 4, 8, 16)
POOL_GROUP = 256
HYENA_WIDTH = 1024
HYENA_EMB_BANDS = 16
HYENA_FAST_DECAY = 0.3
HYENA_SLOW_DECAY = 1.5
HYENA_TARGET = 1e-2
N_HEADS = 16
N_KV_HEADS = 4
HEAD_DIM = 128
GQA_GROUP = 4
WINDOW = 128
BLOCK = 128
N_BUCKETS = 32
MAX_DISTANCE = 128
MASKED = -1e30
HALO = 16
MIB = 1024 * 1024


def _params(sem, vmem_mib):
    return pltpu.CompilerParams(dimension_semantics=sem, vmem_limit_bytes=vmem_mib * MIB)


def _seq_info(r0, lp, ls):
    in_prompt = r0 < lp
    idx = jnp.maximum(r0 - lp, 0) // ls
    start = jnp.where(in_prompt, 0, lp + idx * ls)
    length = jnp.where(in_prompt, lp, ls)
    return start, length


def _rmsnorm(x, g):
    ms = jnp.mean(x * x, axis=-1, keepdims=True)
    return x * lax.rsqrt(ms + EPS) * g


def _ffn_kernel(x_ref, g_ref, wg_ref, wu_ref, wo_ref, o_ref, h_ref):
    j = pl.program_id(1)

    @pl.when(j == 0)
    def _():
        x = x_ref[...]
        h_ref[...] = _rmsnorm(x, g_ref[...]).astype(BF16)
        o_ref[...] = x

    h = h_ref[...]
    gate = jnp.dot(h, wg_ref[...], preferred_element_type=F32)
    up = jnp.dot(h, wu_ref[...], preferred_element_type=F32)
    act = (gate * jax.nn.sigmoid(gate) * (0.5 * up)).astype(BF16)
    o_ref[...] += jnp.dot(act, wo_ref[...], preferred_element_type=F32)


def _ffn(x, g, wi, wo, layer, tm, tf):
    t, d = x.shape
    dff = wo.shape[1]
    nj = dff // tf
    return pl.pallas_call(
        _ffn_kernel,
        grid=(t // tm, nj),
        in_specs=[
            pl.BlockSpec((tm, d), lambda i, j: (i, 0)),
            pl.BlockSpec((1, d), lambda i, j: (0, 0)),
            pl.BlockSpec((None, d, tf), lambda i, j: (layer, 0, j)),
            pl.BlockSpec((None, d, tf), lambda i, j: (layer, 0, j + nj)),
            pl.BlockSpec((None, tf, d), lambda i, j: (layer, j, 0)),
        ],
        out_specs=pl.BlockSpec((tm, d), lambda i, j: (i, 0)),
        out_shape=jax.ShapeDtypeStruct((t, d), F32),
        scratch_shapes=[pltpu.VMEM((tm, d), BF16)],
        compiler_params=_params(("parallel", "arbitrary"), 56),
    )(x, g.reshape(1, d), wi, wi, wo)


def _norm_matmul_kernel(x_ref, g_ref, w_ref, o_ref, h_ref):
    @pl.when(pl.program_id(1) == 0)
    def _():
        h_ref[...] = _rmsnorm(x_ref[...], g_ref[...]).astype(BF16)

    o_ref[...] = jnp.dot(h_ref[...], w_ref[...], preferred_element_type=F32).astype(o_ref.dtype)


def _norm_matmul(x, g, w, layer, tm, tn):
    t, d = x.shape
    n = w.shape[2]
    return pl.pallas_call(
        _norm_matmul_kernel,
        grid=(t // tm, n // tn),
        in_specs=[
            pl.BlockSpec((tm, d), lambda i, j: (i, 0)),
            pl.BlockSpec((1, d), lambda i, j: (0, 0)),
            pl.BlockSpec((None, d, tn), lambda i, j: (layer, 0, j)),
        ],
        out_specs=pl.BlockSpec((tm, tn), lambda i, j: (i, j)),
        out_shape=jax.ShapeDtypeStruct((t, n), BF16),
        scratch_shapes=[pltpu.VMEM((tm, d), BF16)],
        compiler_params=_params(("parallel", "arbitrary"), 48),
    )(x, g.reshape(1, d), w)


def _proj_residual_kernel(*refs):
    x_ref, o_ref = refs[0], refs[-1]
    pairs = refs[1:-1]
    acc = x_ref[...]
    for k in range(len(pairs) // 2):
        acc = acc + jnp.dot(pairs[2 * k][...], pairs[2 * k + 1][...], preferred_element_type=F32)
    o_ref[...] = acc


def _proj_residual(x, ys, w, layer, tm):
    t, d = x.shape
    kw = ys[0].shape[1]
    in_specs = [pl.BlockSpec((tm, d), lambda i: (i, 0))]
    args = [x]
    for k, y in enumerate(ys):
        in_specs.append(pl.BlockSpec((tm, kw), lambda i: (i, 0)))
        in_specs.append(pl.BlockSpec((None, kw, d), lambda i, k=k: (layer, k, 0)))
        args += [y, w]
    return pl.pallas_call(
        _proj_residual_kernel,
        grid=(t // tm,),
        in_specs=in_specs,
        out_specs=pl.BlockSpec((tm, d), lambda i: (i, 0)),
        out_shape=jax.ShapeDtypeStruct((t, d), F32),
        compiler_params=_params(("parallel",), 48),
    )(*args)


def _final_norm_kernel(x_ref, g_ref, o_ref):
    o_ref[...] = _rmsnorm(x_ref[...], g_ref[...])


def _final_norm(x, g, row0, rows, tm):
    d = x.shape[1]
    off = row0 // tm
    return pl.pallas_call(
        _final_norm_kernel,
        grid=(rows // tm,),
        in_specs=[pl.BlockSpec((tm, d), lambda i: (i + off, 0)), pl.BlockSpec((1, d), lambda i: (0, 0))],
        out_specs=pl.BlockSpec((tm, d), lambda i: (i, 0)),
        out_shape=jax.ShapeDtypeStruct((rows, d), F32),
        compiler_params=_params(("parallel",), 32),
    )(x, g.reshape(1, d))


def _pool_kernel(um_ref, up_ref, un_ref, w_ref, sc_ref, o_ref, ext_ref, *, tt, lp, ls):
    r0 = pl.program_id(0) * tt
    start, length = _seq_info(r0, lp, ls)
    first = r0 == start
    last = r0 + tt == start + length
    ext_ref[0:HALO, :] = jnp.where(first, 0.0, up_ref[...].astype(F32))
    ext_ref[HALO:HALO + tt, :] = um_ref[...].astype(F32)
    ext_ref[HALO + tt:, :] = jnp.where(last, 0.0, un_ref[...].astype(F32))
    pos = (r0 - start) + lax.broadcasted_iota(jnp.int32, (tt, 1), 0)
    for g, w in enumerate(POOL_WINDOWS):
        cs = slice(g * POOL_GROUP, (g + 1) * POOL_GROUP)
        s = ext_ref[pl.ds(HALO - w // 2, tt), cs]
        for off in range(-w // 2 + 1, w // 2):
            s = s + ext_ref[pl.ds(HALO + off, tt), cs]
        cnt = (jnp.minimum(pos + w // 2, length) - jnp.maximum(pos - w // 2, 0)).astype(F32)
        p = s / cnt - ext_ref[pl.ds(HALO, tt), cs]
        y = jnp.dot(p.astype(BF16), w_ref[g], preferred_element_type=F32) * sc_ref[:, cs]
        o_ref[:, cs] = y.astype(BF16)


def _pool(u, pool_w, scale, tt, lp, ls):
    t = u.shape[0]
    nh = t // HALO
    hb = tt // HALO
    return pl.pallas_call(
        functools.partial(_pool_kernel, tt=tt, lp=lp, ls=ls),
        grid=(t // tt,),
        in_specs=[
            pl.BlockSpec((tt, POOL_WIDTH), lambda i: (i, 0)),
            pl.BlockSpec((HALO, POOL_WIDTH), lambda i: (jnp.maximum(i * hb - 1, 0), 0)),
            pl.BlockSpec((HALO, POOL_WIDTH), lambda i: (jnp.minimum((i + 1) * hb, nh - 1), 0)),
            pl.BlockSpec((len(POOL_WINDOWS), POOL_GROUP, POOL_GROUP), lambda i: (0, 0, 0)),
            pl.BlockSpec((1, POOL_WIDTH), lambda i: (0, 0)),
        ],
        out_specs=pl.BlockSpec((tt, POOL_WIDTH), lambda i: (i, 0)),
        out_shape=jax.ShapeDtypeStruct((t, POOL_WIDTH), BF16),
        scratch_shapes=[pltpu.VMEM((tt + 2 * HALO, POOL_WIDTH), F32)],
        compiler_params=_params(("parallel",), 32),
    )(u, u, u, pool_w, scale.reshape(1, POOL_WIDTH))


def _gate_kernel(m0, m1, m2, p0, p1, p2, n0, n1, n2, cw_ref, cb_ref,
                 vvb_ref, vvf_ref, x0_ref, nyq_ref, *, tt, lp, ls):
    r0 = pl.program_id(0) * tt
    start, length = _seq_info(r0, lp, ls)
    first = r0 == start
    last = r0 + tt == start + length
    row = lax.broadcasted_iota(jnp.int32, (tt, 1), 0)

    def conv(m_ref, p_ref, n_ref, q):
        cs = slice(q * HYENA_WIDTH, (q + 1) * HYENA_WIDTH)
        m = m_ref[...].astype(F32)
        before = jnp.where(first, 0.0, p_ref[...].astype(F32)[HALO - 1:HALO, :])
        after = jnp.where(last, 0.0, n_ref[...].astype(F32)[0:1, :])
        up = jnp.where(row == 0, before, pltpu.roll(m, 1, 0))
        dn = jnp.where(row == tt - 1, after, pltpu.roll(m, tt - 1, 0))
        return up * cw_ref[0:1, cs] + m * cw_ref[1:2, cs] + dn * cw_ref[2:3, cs] + cb_ref[:, cs]

    x0 = conv(m0, p0, n0, 0)
    x1 = conv(m1, p1, n1, 1)
    v = conv(m2, p2, n2, 2)
    vv = v * x1
    vvb_ref[...] = vv.astype(BF16)
    vvf_ref[...] = vv
    x0_ref[...] = x0
    sign = jnp.where((row & 1) == 0, 1.0, -1.0)
    part = (vv * sign).reshape(tt // 8, 8, HYENA_WIDTH).sum(axis=0)

    @pl.when(first)
    def _():
        nyq_ref[...] = part

    @pl.when(jnp.logical_not(first))
    def _():
        nyq_ref[...] += part


def _gate(u, conv_w, conv_b, tt, lp, ls, n_seq):
    t = u.shape[0]
    nh = t // HALO
    hb = tt // HALO
    c = HYENA_WIDTH

    def seq_of(i):
        r0 = i * tt
        return jnp.where(r0 < lp, 0, 1 + jnp.maximum(r0 - lp, 0) // ls)

    main = [pl.BlockSpec((tt, c), lambda i, q=q: (i, 1 + q)) for q in range(3)]
    prev = [pl.BlockSpec((HALO, c), lambda i, q=q: (jnp.maximum(i * hb - 1, 0), 1 + q)) for q in range(3)]
    nxt = [pl.BlockSpec((HALO, c), lambda i, q=q: (jnp.minimum((i + 1) * hb, nh - 1), 1 + q)) for q in range(3)]
    return pl.pallas_call(
        functools.partial(_gate_kernel, tt=tt, lp=lp, ls=ls),
        grid=(t // tt,),
        in_specs=main + prev + nxt + [
            pl.BlockSpec((3, 3 * c), lambda i: (0, 0)),
            pl.BlockSpec((1, 3 * c), lambda i: (0, 0)),
        ],
        out_specs=[
            pl.BlockSpec((tt, c), lambda i: (i, 0)),
            pl.BlockSpec((tt, c), lambda i: (i, 0)),
            pl.BlockSpec((tt, c), lambda i: (i, 0)),
            pl.BlockSpec((None, 8, c), lambda i: (seq_of(i), 0, 0)),
        ],
        out_shape=[
            jax.ShapeDtypeStruct((t, c), BF16),
            jax.ShapeDtypeStruct((t, c), F32),
            jax.ShapeDtypeStruct((t, c), F32),
            jax.ShapeDtypeStruct((n_seq, 8, c), F32),
        ],
        compiler_params=_params(("arbitrary",), 48),
    )(*([u] * 9), conv_w, conv_b.reshape(1, 3 * c))


def _filter_kernel(z_ref, w1, b1, w2, b2, w3, b3, w4, b4, fr_ref, dl_ref, e_ref, o_ref, st_ref, *, tr):
    i = pl.program_id(0)
    z = z_ref[...]
    fr = fr_ref[...]

    def dot(a, b):
        return jnp.dot(a, b, preferred_element_type=F32, precision=HIGHEST)

    h = jnp.sin(fr * (dot(z, w1[...]) + b1[...]))
    h = jnp.sin(fr * (dot(h, w2[...]) + b2[...]))
    h = jnp.sin(fr * (dot(h, w3[...]) + b3[...]))
    hh = dot(h, w4[...]) + b4[...]
    decay = jnp.exp(-z[:, 0:1] * jnp.abs(dl_ref[...]))
    row = i * tr + lax.broadcasted_iota(jnp.int32, (tr, 1), 0)
    hf = hh[:, :HYENA_WIDTH] * decay
    hb = jnp.where(row == 0, 0.0, hh[:, HYENA_WIDTH:] * decay)
    e = hf + hb
    e_ref[...] = e.astype(e_ref.dtype)
    o_ref[...] = (hf - hb).astype(o_ref.dtype)
    sign = jnp.where((row & 1) == 0, 1.0, -1.0)
    norm_part = (jnp.abs(hf) + jnp.abs(hb)).reshape(tr // 8, 8, HYENA_WIDTH).sum(axis=0)
    nyq_part = (e * sign).reshape(tr // 8, 8, HYENA_WIDTH).sum(axis=0)

    @pl.when(i == 0)
    def _():
        st_ref[0] = norm_part
        st_ref[1] = nyq_part

    @pl.when(i > 0)
    def _():
        st_ref[0] += norm_part
        st_ref[1] += nyq_part


def _hyena_filter(length, fp, tr, dtype):
    c = HYENA_WIDTH
    t = jnp.linspace(0.0, 1.0, length, dtype=F32)[:, None]
    w = 2.0 * math.pi * jnp.arange(length, dtype=F32)[:, None] / length
    f = jnp.linspace(1e-4, HYENA_EMB_BANDS - 1, HYENA_EMB_BANDS, dtype=F32)[None, :]
    z = jnp.concatenate([t, jnp.cos(f * w), -jnp.sin(f * w)], axis=-1)
    z = jnp.pad(z, ((0, 0), (0, 128 - z.shape[1])))
    min_decay = math.log(HYENA_TARGET) / HYENA_SLOW_DECAY
    max_decay = math.log(HYENA_TARGET) / HYENA_FAST_DECAY
    deltas = jnp.linspace(min_decay, max_decay, c, dtype=F32)[None, :]
    full = lambda shape: pl.BlockSpec(shape, lambda i: (0,) * len(shape))
    return pl.pallas_call(
        functools.partial(_filter_kernel, tr=tr),
        grid=(length // tr,),
        in_specs=[pl.BlockSpec((tr, 128), lambda i: (i, 0))] + [full(a.shape) for a in fp] + [full((1, c))],
        out_specs=[
            pl.BlockSpec((tr, c), lambda i: (i, 0)),
            pl.BlockSpec((tr, c), lambda i: (i, 0)),
            full((2, 8, c)),
        ],
        out_shape=[
            jax.ShapeDtypeStruct((length, c), dtype),
            jax.ShapeDtypeStruct((length, c), dtype),
            jax.ShapeDtypeStruct((2, 8, c), F32),
        ],
        compiler_params=_params(("arbitrary",), 48),
    )(z, *fp, deltas)


def _pad_filter_params(w1, b1, w2, b2, w3, b3, w4, b4, freq):
    hid = 128
    pad2 = lambda a, r, c: jnp.pad(a.astype(F32), ((0, r - a.shape[0]), (0, c - a.shape[1])))
    row = lambda a: jnp.pad(a.astype(F32), (0, hid - a.shape[0])).reshape(1, hid)
    return (pad2(w1, 128, hid), row(b1), pad2(w2, hid, hid), row(b2), pad2(w3, hid, hid), row(b3),
            pad2(w4, hid, w4.shape[1]), b4.astype(F32).reshape(1, -1), row(freq))


def _dft_mats(length):
    n = 2 * length
    k = jnp.arange(length, dtype=jnp.int32)[:, None]

    def table(cols):
        ang = ((k * cols[None, :]) % n).astype(F32) * (2.0 * math.pi / n)
        return jnp.cos(ang), jnp.sin(ang)

    ca, sa = table(jnp.arange(length // 128, dtype=jnp.int32) * 128)
    cb, sb = table(jnp.arange(128, dtype=jnp.int32))
    cos = ca[:, :, None] * cb[:, None, :] - sa[:, :, None] * sb[:, None, :]
    sin = sa[:, :, None] * cb[:, None, :] + ca[:, :, None] * sb[:, None, :]
    return cos.reshape(length, length).astype(BF16), sin.reshape(length, length).astype(BF16)


def _spectrum_scale(st_ref, k0, tm, length):
    norm = jnp.sum(st_ref[0], axis=0, keepdims=True)
    krow = k0 + lax.broadcasted_iota(jnp.int32, (tm, 1), 0)
    wk = jnp.where(krow == 0, 1.0, 2.0)
    return wk / (2.0 * length * norm)


def _dft_filter_kernel(c_ref, s_ref, e_ref, o_ref, st_ref, kr_ref, ki_ref, accr, acci, *, tm, length):
    kt = pl.program_id(1)

    @pl.when(kt == 0)
    def _():
        accr[...] = jnp.zeros_like(accr)
        acci[...] = jnp.zeros_like(acci)

    accr[...] += jnp.dot(c_ref[...], e_ref[...], preferred_element_type=F32)
    acci[...] += jnp.dot(s_ref[...], o_ref[...], preferred_element_type=F32)

    @pl.when(kt == pl.num_programs(1) - 1)
    def _():
        sc = _spectrum_scale(st_ref, pl.program_id(0) * tm, tm, length)
        kr_ref[...] = accr[...] * sc
        ki_ref[...] = -acci[...] * sc


def _dft_filter(cm, sm, e, o, st, tm, tk):
    length = cm.shape[0]
    c = HYENA_WIDTH
    return pl.pallas_call(
        functools.partial(_dft_filter_kernel, tm=tm, length=length),
        grid=(length // tm, length // tk),
        in_specs=[
            pl.BlockSpec((tm, tk), lambda m, k: (m, k)),
            pl.BlockSpec((tm, tk), lambda m, k: (m, k)),
            pl.BlockSpec((tk, c), lambda m, k: (k, 0)),
            pl.BlockSpec((tk, c), lambda m, k: (k, 0)),
            pl.BlockSpec((2, 8, c), lambda m, k: (0, 0, 0)),
        ],
        out_specs=[pl.BlockSpec((tm, c), lambda m, k: (m, 0))] * 2,
        out_shape=[jax.ShapeDtypeStruct((length, c), F32)] * 2,
        scratch_shapes=[pltpu.VMEM((tm, c), F32)] * 2,
        compiler_params=_params(("parallel", "arbitrary"), 48),
    )(cm, sm, e, o, st)


def _dft_fwd_kernel(c_ref, s_ref, v_ref, kr_ref, ki_ref, yr_ref, yi_ref, accr, acci):
    kt = pl.program_id(2)

    @pl.when(kt == 0)
    def _():
        accr[...] = jnp.zeros_like(accr)
        acci[...] = jnp.zeros_like(acci)

    v = v_ref[...]
    accr[...] += jnp.dot(c_ref[...], v, preferred_element_type=F32)
    acci[...] += jnp.dot(s_ref[...], v, preferred_element_type=F32)

    @pl.when(kt == pl.num_programs(2) - 1)
    def _():
        vr = accr[...]
        vi = -acci[...]
        kr = kr_ref[...]
        ki = ki_ref[...]
        yr_ref[...] = (vr * kr - vi * ki).astype(BF16)
        yi_ref[...] = (vr * ki + vi * kr).astype(BF16)


def _dft_fwd(cm, sm, vvb, kr, ki, row0, batch, tm, tk):
    length = cm.shape[0]
    c = HYENA_WIDTH
    nm, nk = length // tm, length // tk
    off_k = row0 // tk
    return pl.pallas_call(
        _dft_fwd_kernel,
        grid=(nm, batch, nk),
        in_specs=[
            pl.BlockSpec((tm, tk), lambda m, b, k: (m, k)),
            pl.BlockSpec((tm, tk), lambda m, b, k: (m, k)),
            pl.BlockSpec((tk, c), lambda m, b, k: (off_k + b * nk + k, 0)),
            pl.BlockSpec((tm, c), lambda m, b, k: (m, 0)),
            pl.BlockSpec((tm, c), lambda m, b, k: (m, 0)),
        ],
        out_specs=[pl.BlockSpec((tm, c), lambda m, b, k: (b * nm + m, 0))] * 2,
        out_shape=[jax.ShapeDtypeStruct((batch * length, c), BF16)] * 2,
        scratch_shapes=[pltpu.VMEM((tm, c), F32)] * 2,
        compiler_params=_params(("parallel", "parallel", "arbitrary"), 48),
    )(cm, sm, vvb, kr, ki)


def _dft_inv_kernel(c_ref, s_ref, yr_ref, yi_ref, st_ref, vn_ref, vv_ref, x0_ref, d_ref, o_ref, acc,
                    *, tm, length):
    kt = pl.program_id(2)

    @pl.when(kt == 0)
    def _():
        acc[...] = jnp.zeros_like(acc)

    acc[...] += (jnp.dot(c_ref[...], yr_ref[...], preferred_element_type=F32)
                 - jnp.dot(s_ref[...], yi_ref[...], preferred_element_type=F32))

    @pl.when(kt == pl.num_programs(2) - 1)
    def _():
        norm = jnp.sum(st_ref[0], axis=0, keepdims=True)
        knyq = jnp.sum(st_ref[1], axis=0, keepdims=True)
        vnyq = jnp.sum(vn_ref[...], axis=0, keepdims=True)
        ynyq = vnyq * knyq / (2.0 * length * norm)
        row = lax.broadcasted_iota(jnp.int32, (tm, 1), 0)
        sign = jnp.where((row & 1) == 0, 1.0, -1.0)
        y = acc[...] + sign * ynyq
        o_ref[...] = ((y + vv_ref[...] * d_ref[...]) * x0_ref[...]).astype(BF16)


def _dft_inv_aliased_kernel(base_ref, *refs, tm, length):
    del base_ref
    _dft_inv_kernel(*refs, tm=tm, length=length)


def _dft_inv(cm, sm, yr, yi, st, vnyq, vvf, x0, d, row0, batch, tm, tk, base=None):
    length = cm.shape[0]
    c = HYENA_WIDTH
    nm, nk = length // tm, length // tk
    off_m = row0 // tm
    seq0 = 0 if row0 == 0 else 1
    body = _dft_inv_kernel if base is None else _dft_inv_aliased_kernel
    lead_specs = [] if base is None else [pl.BlockSpec(memory_space=pl.ANY)]
    lead_args = [] if base is None else [base]
    return pl.pallas_call(
        functools.partial(body, tm=tm, length=length),
        grid=(nm, batch, nk),
        input_output_aliases={} if base is None else {0: 0},
        in_specs=lead_specs + [
            pl.BlockSpec((tm, tk), lambda m, b, k: (m, k)),
            pl.BlockSpec((tm, tk), lambda m, b, k: (m, k)),
            pl.BlockSpec((tk, c), lambda m, b, k: (b * nk + k, 0)),
            pl.BlockSpec((tk, c), lambda m, b, k: (b * nk + k, 0)),
            pl.BlockSpec((2, 8, c), lambda m, b, k: (0, 0, 0)),
            pl.BlockSpec((None, 8, c), lambda m, b, k: (seq0 + b, 0, 0)),
            pl.BlockSpec((tm, c), lambda m, b, k: (off_m + b * nm + m, 0)),
            pl.BlockSpec((tm, c), lambda m, b, k: (off_m + b * nm + m, 0)),
            pl.BlockSpec((1, c), lambda m, b, k: (0, 0)),
        ],
        out_specs=pl.BlockSpec((tm, c), lambda m, b, k: (off_m + b * nm + m, 0)),
        out_shape=jax.ShapeDtypeStruct((vvf.shape[0], c), BF16),
        scratch_shapes=[pltpu.VMEM((tm, c), F32)],
        compiler_params=_params(("parallel", "parallel", "arbitrary"), 48),
    )(*lead_args, cm, sm, yr, yi, st, vnyq, vvf, x0, d.reshape(1, c))


DFT_INNER = 128
TWO_STAGE_MIN_LEN = 4096


def _two_stage_tables(length):
    n = 2 * length
    n1 = n // DFT_INNER
    a = jnp.arange(n1, dtype=jnp.int32)
    ang1 = ((a[:, None] * a[None, :n1 // 2]) % n1).astype(F32) * (2.0 * math.pi / n1)
    c1, s1 = jnp.cos(ang1), jnp.sin(ang1)
    f1 = jnp.concatenate([c1, -s1], axis=0).astype(BF16)
    f3 = jnp.concatenate([c1.T, -s1.T], axis=1).astype(BF16)
    b = jnp.arange(DFT_INNER, dtype=jnp.int32)
    k = a[:, None, None] + n1 * b[None, :, None]
    ang2 = ((k * b[None, None, :]) % n).astype(F32) * (2.0 * math.pi / n)
    c2, s2 = jnp.cos(ang2), jnp.sin(ang2)
    m1 = jnp.concatenate([jnp.concatenate([c2, s2], axis=2), jnp.concatenate([-s2, c2], axis=2)], axis=1)
    return f1, f3, m1.astype(BF16), jnp.swapaxes(m1, 1, 2).astype(BF16)


def _s1_kernel(f_ref, x_ref, a_ref, *, tn):
    for j in range(tn):
        xj = x_ref[:, j, :].astype(BF16)
        a_ref[:, j, :] = jnp.dot(f_ref[...], xj, preferred_element_type=F32)


def _stage1(f1, x3, tn):
    rows2, half = f1.shape
    c = x3.shape[2]
    return pl.pallas_call(
        functools.partial(_s1_kernel, tn=tn),
        grid=(DFT_INNER // tn,),
        in_specs=[pl.BlockSpec((rows2, half), lambda i: (0, 0)), pl.BlockSpec((half, tn, c), lambda i: (0, i, 0))],
        out_specs=pl.BlockSpec((rows2, tn, c), lambda i: (0, i, 0)),
        out_shape=jax.ShapeDtypeStruct((rows2, DFT_INNER, c), F32),
        compiler_params=_params(("parallel",), 40),
    )(f1, x3)


def _stacked(re_ref, im_ref, q):
    return jnp.concatenate([re_ref[q], im_ref[q]], axis=0).astype(BF16)


def _mid_filter_kernel(er_ref, ei_ref, or_ref, oi_ref, m1_ref, st_ref, k_ref, *, kb, length):
    scale = 1.0 / (2.0 * length * jnp.sum(st_ref[0], axis=0, keepdims=True))
    for q in range(kb):
        xe = jnp.dot(m1_ref[q], _stacked(er_ref, ei_ref, q), preferred_element_type=F32)
        xo = jnp.dot(m1_ref[q], _stacked(or_ref, oi_ref, q), preferred_element_type=F32)
        k_ref[0, q] = xe[:DFT_INNER] * scale
        k_ref[1, q] = xo[DFT_INNER:] * scale


def _mid_filter(ae, ao, m1, st, kb, ct):
    n1 = m1.shape[0]
    c = HYENA_WIDTH
    nb = n1 // kb
    re = pl.BlockSpec((kb, DFT_INNER, ct), lambda i, j: (i, 0, j))
    im = pl.BlockSpec((kb, DFT_INNER, ct), lambda i, j: (nb + i, 0, j))
    return pl.pallas_call(
        functools.partial(_mid_filter_kernel, kb=kb, length=n1 * DFT_INNER // 2),
        grid=(nb, c // ct),
        in_specs=[re, im, re, im,
                  pl.BlockSpec((kb, 2 * DFT_INNER, 2 * DFT_INNER), lambda i, j: (i, 0, 0)),
                  pl.BlockSpec((2, 8, ct), lambda i, j: (0, 0, j))],
        out_specs=pl.BlockSpec((2, kb, DFT_INNER, ct), lambda i, j: (0, i, 0, j)),
        out_shape=jax.ShapeDtypeStruct((2, n1, DFT_INNER, c), F32),
        compiler_params=_params(("parallel", "parallel"), 40),
    )(ae, ae, ao, ao, m1, st)


def _mid_kernel(ar_ref, ai_ref, m1_ref, m2_ref, k_ref, zr_ref, zi_ref, *, kb):
    for q in range(kb):
        x = jnp.dot(m1_ref[q], _stacked(ar_ref, ai_ref, q), preferred_element_type=F32)
        xr, xi = x[:DFT_INNER], x[DFT_INNER:]
        kr, ki = k_ref[0, q], k_ref[1, q]
        y = jnp.concatenate([xr * kr - xi * ki, xr * ki + xi * kr], axis=0).astype(BF16)
        z = jnp.dot(m2_ref[q], y, preferred_element_type=F32)
        zr_ref[:, q, :] = z[:DFT_INNER]
        zi_ref[:, q, :] = z[DFT_INNER:]


def _mid(a, m1, m2, kf, kb, ct):
    n1 = m1.shape[0]
    c = HYENA_WIDTH
    nb = n1 // kb
    mat = pl.BlockSpec((kb, 2 * DFT_INNER, 2 * DFT_INNER), lambda i, j: (i, 0, 0))
    zt = pl.BlockSpec((DFT_INNER, kb, ct), lambda i, j: (0, i, j))
    return pl.pallas_call(
        functools.partial(_mid_kernel, kb=kb),
        grid=(nb, c // ct),
        in_specs=[pl.BlockSpec((kb, DFT_INNER, ct), lambda i, j: (i, 0, j)),
                  pl.BlockSpec((kb, DFT_INNER, ct), lambda i, j: (nb + i, 0, j)),
                  mat, mat,
                  pl.BlockSpec((2, kb, DFT_INNER, ct), lambda i, j: (0, i, 0, j))],
        out_specs=[zt, zt],
        out_shape=[jax.ShapeDtypeStruct((DFT_INNER, n1, c), F32)] * 2,
        compiler_params=_params(("parallel", "parallel"), 40),
    )(a, a, m1, m2, kf)


def _s3_kernel(f_ref, zr_ref, zi_ref, y_ref, *, tn):
    for j in range(tn):
        y_ref[:, j, :] = jnp.dot(f_ref[...], _stacked(zr_ref, zi_ref, j), preferred_element_type=F32)


def _stage3(f3, zr, zi, tn):
    half, rows2 = f3.shape
    n1, c = zr.shape[1], zr.shape[2]
    zs = pl.BlockSpec((tn, n1, c), lambda i: (i, 0, 0))
    return pl.pallas_call(
        functools.partial(_s3_kernel, tn=tn),
        grid=(DFT_INNER // tn,),
        in_specs=[pl.BlockSpec((half, rows2), lambda i: (0, 0)), zs, zs],
        out_specs=pl.BlockSpec((half, tn, c), lambda i: (0, i, 0)),
        out_shape=jax.ShapeDtypeStruct((half, DFT_INNER, c), F32),
        compiler_params=_params(("parallel",), 40),
    )(f3, zr, zi)


def _hy_out_kernel(y_ref, vv_ref, x0_ref, d_ref, o_ref):
    o_ref[...] = ((y_ref[...] + vv_ref[...] * d_ref[...]) * x0_ref[...]).astype(BF16)


def _hy_out(y, vvf, x0, d, tt):
    c = HYENA_WIDTH
    row = pl.BlockSpec((tt, c), lambda i: (i, 0))
    return pl.pallas_call(
        _hy_out_kernel,
        grid=(y.shape[0] // tt,),
        in_specs=[row, row, row, pl.BlockSpec((1, c), lambda i: (0, 0))],
        out_specs=row,
        out_shape=jax.ShapeDtypeStruct((vvf.shape[0], c), BF16),
        compiler_params=_params(("parallel",), 32),
    )(y, vvf, x0, d.reshape(1, c))


def _t5_buckets(rel):
    nb = N_BUCKETS // 2
    max_exact = nb // 2
    ret = (rel > 0).astype(jnp.int32) * nb
    n = jnp.abs(rel)
    large = max_exact + (jnp.log(jnp.maximum(n, 1).astype(F32) / max_exact)
                         / math.log(MAX_DISTANCE / max_exact) * (nb - max_exact)).astype(jnp.int32)
    large = jnp.minimum(large, nb - 1)
    return ret + jnp.where(n < max_exact, n, large)


def _bias_kernel(bk_ref, rb_ref, o_ref):
    h = pl.program_id(0)
    bk = bk_ref[...]
    qi = lax.broadcasted_iota(jnp.int32, bk.shape, 0)
    pj = lax.broadcasted_iota(jnp.int32, bk.shape, 1) - BLOCK
    acc = jnp.zeros(bk.shape, F32)
    for b in range(N_BUCKETS):
        acc = jnp.where(bk == b, rb_ref[b, h], acc)
    o_ref[...] = jnp.where(jnp.abs(pj - qi) <= WINDOW, acc, MASKED)


def _attn_bias(rel_bias):
    qi = jnp.arange(BLOCK)[:, None]
    pj = jnp.arange(3 * BLOCK)[None, :] - BLOCK
    buckets = _t5_buckets(pj - qi).astype(jnp.int32)
    return pl.pallas_call(
        _bias_kernel,
        grid=(N_HEADS,),
        in_specs=[
            pl.BlockSpec((BLOCK, 3 * BLOCK), lambda h: (0, 0)),
            pl.BlockSpec(memory_space=pltpu.SMEM),
        ],
        out_specs=pl.BlockSpec((None, BLOCK, 3 * BLOCK), lambda h: (h, 0, 0)),
        out_shape=jax.ShapeDtypeStruct((N_HEADS, BLOCK, 3 * BLOCK), F32),
        compiler_params=_params(("arbitrary",), 16),
    )(buckets, rel_bias.astype(F32))


def _attn_kernel(q_ref, kp_ref, kc_ref, kn_ref, vp_ref, vc_ref, vn_ref, bias_ref, sink_ref, o_ref, *, lp, ls):
    r0 = pl.program_id(0) * BLOCK
    start, length = _seq_info(r0, lp, ls)
    first = r0 == start
    last = r0 + BLOCK == start + length
    rows = GQA_GROUP * BLOCK
    col = lax.broadcasted_iota(jnp.int32, (rows, 3 * BLOCK), 1)
    outside = (first & (col < BLOCK)) | (last & (col >= 2 * BLOCK))
    grp = lax.broadcasted_iota(jnp.int32, (rows, 1), 0) // BLOCK
    scale = 1.0 / math.sqrt(HEAD_DIM)
    for kh in range(N_KV_HEADS):
        hs = slice(kh * HEAD_DIM, (kh + 1) * HEAD_DIM)
        kcat = jnp.concatenate([kp_ref[:, hs], kc_ref[:, hs], kn_ref[:, hs]], axis=0)
        vcat = jnp.concatenate([vp_ref[:, hs], vc_ref[:, hs], vn_ref[:, hs]], axis=0)
        heads = [kh * GQA_GROUP + g for g in range(GQA_GROUP)]
        q4 = jnp.concatenate([q_ref[:, h * HEAD_DIM:(h + 1) * HEAD_DIM] for h in heads], axis=0)
        s = lax.dot_general(q4, kcat, (((1,), (1,)), ((), ())), preferred_element_type=F32) * scale
        s = jnp.where(outside, MASKED, s + bias_ref[kh])
        sink = jnp.zeros((rows, 1), F32)
        for g, h in enumerate(heads):
            sink = jnp.where(grp == g, sink_ref[h], sink)
        m = jnp.maximum(jnp.max(s, axis=-1, keepdims=True), sink)
        p = jnp.exp(s - m)
        denom = jnp.sum(p, axis=-1, keepdims=True) + jnp.exp(sink - m)
        o = jnp.dot(p.astype(BF16), vcat, preferred_element_type=F32) / denom
        for g, h in enumerate(heads):
            o_ref[:, h * HEAD_DIM:(h + 1) * HEAD_DIM] = o[g * BLOCK:(g + 1) * BLOCK].astype(BF16)


def _attention(qkv, bias, sink, lp, ls):
    t = qkv.shape[0]
    nb = t // BLOCK
    qw = N_HEADS * HEAD_DIM
    kvw = N_KV_HEADS * HEAD_DIM
    kcol = qw // kvw
    prev = lambda i: jnp.maximum(i - 1, 0)
    nxt = lambda i: jnp.minimum(i + 1, nb - 1)
    kv_specs = [pl.BlockSpec((BLOCK, kvw), lambda i, f=f, cb=cb: (f(i), cb))
                for cb in (kcol, kcol + 1) for f in (prev, lambda i: i, nxt)]
    return pl.pallas_call(
        functools.partial(_attn_kernel, lp=lp, ls=ls),
        grid=(nb,),
        in_specs=[pl.BlockSpec((BLOCK, qw), lambda i: (i, 0))] + kv_specs + [
            pl.BlockSpec((N_KV_HEADS, GQA_GROUP * BLOCK, 3 * BLOCK), lambda i: (0, 0, 0)),
            pl.BlockSpec(memory_space=pltpu.SMEM),
        ],
        out_specs=pl.BlockSpec((BLOCK, qw), lambda i: (i, 0)),
        out_shape=jax.ShapeDtypeStruct((t, qw), BF16),
        compiler_params=_params(("parallel",), 32),
    )(qkv, *([qkv] * 6), bias.reshape(N_KV_HEADS, GQA_GROUP * BLOCK, 3 * BLOCK), sink.astype(F32))


def _tiles(lp, ls, dff):
    tok = min(512, ls)
    return dict(
        tm=tok,
        fm=min(1024, ls),
        tf=512 if dff % 512 == 0 else dff,
        tn=min(1024, ls),
        tt=min(512, ls),
        dm=min(512, ls),
        dk=min(2048, ls),
        tn2=8,
        kb=8,
        ct=512,
    )


def _forward(x_prompt, x_sample, p):
    depth = p['norm_ffn1'].shape[0]
    lp = x_prompt.shape[0] * x_prompt.shape[1]
    assert x_prompt.shape[0] == 1
    bs, ls = x_sample.shape[0], x_sample.shape[1]
    n_seq = 1 + bs
    assert lp % ls == 0
    dff = p['ffn1_wo'].shape[1]
    tl = _tiles(lp, ls, dff)
    tm, tt = tl['tm'], tl['tt']

    x = jnp.concatenate([x_prompt.reshape(lp, D_MODEL), x_sample.reshape(bs * ls, D_MODEL)], axis=0)
    bf = lambda a: a.astype(BF16)
    ffn1_wi, ffn1_wo = bf(p['ffn1_wi']), bf(p['ffn1_wo'])
    ffn2_wi, ffn2_wo = bf(p['ffn2_wi']), bf(p['ffn2_wo'])
    ab_w_in, ab_w_out, pool_w = bf(p['ab_w_in']), bf(p['ab_w_out']), bf(p['pool_w'])
    w_qkv, w_o = bf(p['attn_w_qkv']), bf(p['attn_w_o'])
    bias = _attn_bias(p['rel_bias'])
    segments = ((0, 1, lp), (lp, bs, ls))
    lengths = {lp, ls}
    tables = {length: _two_stage_tables(length) for length in lengths if length >= TWO_STAGE_MIN_LEN}
    mats = {length: _dft_mats(length) for length in lengths if length < TWO_STAGE_MIN_LEN}

    for layer in range(depth):
        x = _ffn(x, p['norm_ffn1'][layer], ffn1_wi, ffn1_wo, layer, tl['fm'], tl['tf'])
        i = layer // 2
        if layer % 2 == 0:
            u = _norm_matmul(x, p['norm_mix'][layer], ab_w_in, i, tl['fm'], tl['tn'])
            ya = _pool(u, pool_w[i], p['pool_scale'][i], tt, lp, ls)
            vvb, vvf, x0, vnyq = _gate(u, p['hy_conv_w'][i].astype(F32), p['hy_conv_b'][i].astype(F32),
                                       tt, lp, ls, n_seq)
            fp = _pad_filter_params(p['hy_ff_w1'][i], p['hy_ff_b1'][i], p['hy_ff_w2'][i], p['hy_ff_b2'][i],
                                    p['hy_ff_w3'][i], p['hy_ff_b3'][i], p['hy_ff_w4'][i], p['hy_ff_b4'][i],
                                    p['hy_freq'][i])
            d = p['hy_d'][i].astype(F32)
            yb = None
            for row0, batch, length in segments:
                two_stage = length in tables
                e, o, st = _hyena_filter(length, fp, min(512, length), F32 if two_stage else BF16)
                if two_stage:
                    assert row0 == 0 and batch == 1 and yb is None
                    f1, f3, m1, m2 = tables[length]
                    slabs = lambda a: a.reshape(-1, DFT_INNER, HYENA_WIDTH)
                    tn, kb, ct = tl['tn2'], tl['kb'], tl['ct']
                    kf = _mid_filter(_stage1(f1, slabs(e), tn), _stage1(f1, slabs(o), tn), m1, st, kb, ct)
                    zr, zi = _mid(_stage1(f1, slabs(vvf), tn), m1, m2, kf, kb, ct)
                    y = _stage3(f3, zr, zi, tn).reshape(length, HYENA_WIDTH)
                    yb = _hy_out(y, vvf, x0, d, tt)
                else:
                    cm, sm = mats[length]
                    dm, dk = min(tl['dm'], length), min(tl['dk'], length)
                    kr, ki = _dft_filter(cm, sm, e, o, st, dm, dk)
                    yr, yi = _dft_fwd(cm, sm, vvb, kr, ki, row0, batch, dm, dk)
                    yb = _dft_inv(cm, sm, yr, yi, st, vnyq, vvf, x0, d, row0, batch, dm, dk, base=yb)
            x = _proj_residual(x, [ya, yb], ab_w_out, i, tm)
        else:
            qkv = _norm_matmul(x, p['norm_mix'][layer], w_qkv, i, tl['fm'], tl['tn'])
            o = _attention(qkv, bias, p['attn_sink'][i], lp, ls)
            x = _proj_residual(x, [o], w_o, i, tm)
        x = _ffn(x, p['norm_ffn2'][layer], ffn2_wi, ffn2_wo, layer, tl['fm'], tl['tf'])

    y_prompt = _final_norm(x, p['norm_final'], 0, lp, tm).reshape(x_prompt.shape)
    y_sample = _final_norm(x, p['norm_final'], lp, bs * ls, tm).reshape(x_sample.shape)
    return y_prompt, y_sample


def kernel(x_prompt, x_sample, norm_ffn1, ffn1_wi, ffn1_wo, norm_mix, ab_w_in, pool_w, pool_scale, hy_conv_w, hy_conv_b, hy_ff_w1, hy_ff_b1, hy_ff_w2, hy_ff_b2, hy_ff_w3, hy_ff_b3, hy_ff_w4, hy_ff_b4, hy_freq, hy_d, ab_w_out, attn_w_qkv, attn_w_o, attn_sink, rel_bias, norm_ffn2, ffn2_wi, ffn2_wo, norm_final):
    p = dict(norm_ffn1=norm_ffn1, ffn1_wi=ffn1_wi, ffn1_wo=ffn1_wo, norm_mix=norm_mix,
             ab_w_in=ab_w_in, pool_w=pool_w, pool_scale=pool_scale,
             hy_conv_w=hy_conv_w, hy_conv_b=hy_conv_b, hy_ff_w1=hy_ff_w1, hy_ff_b1=hy_ff_b1,
             hy_ff_w2=hy_ff_w2, hy_ff_b2=hy_ff_b2, hy_ff_w3=hy_ff_w3, hy_ff_b3=hy_ff_b3,
             hy_ff_w4=hy_ff_w4, hy_ff_b4=hy_ff_b4, hy_freq=hy_freq, hy_d=hy_d, ab_w_out=ab_w_out,
             attn_w_qkv=attn_w_qkv, attn_w_o=attn_w_o, attn_sink=attn_sink, rel_bias=rel_bias,
             norm_ffn2=norm_ffn2, ffn2_wi=ffn2_wi, ffn2_wo=ffn2_wo, norm_final=norm_final)
    return _forward(x_prompt, x_sample, p)
```

```python
import functools
import math

import jax
import jax.numpy as jnp
import numpy as np
from jax import lax
from jax.experimental import pallas as pl
from jax.experimental.pallas import tpu as pltpu

F32 = jnp.float32
BF16 = jnp.bfloat16
HIGHEST = lax.Precision.HIGHEST

D_MODEL = 2048
EPS = 1e-6
POOL_WIDTH = 1024
POOL_WINDOWS = (2, 4, 8, 16)
POOL_GROUP = 256
HYENA_WIDTH = 1024
HYENA_EMB_BANDS = 16
HYENA_FAST_DECAY = 0.3
HYENA_SLOW_DECAY = 1.5
HYENA_TARGET = 1e-2
N_HEADS = 16
N_KV_HEADS = 4
HEAD_DIM = 128
GQA_GROUP = 4
WINDOW = 128
BLOCK = 128
N_BUCKETS = 32
MAX_DISTANCE = 128
MASKED = -1e30
HALO = 16
MIB = 1024 * 1024
DFT_INNER = 128
TWO_STAGE_MIN_LEN = 4096


def _params(sem, vmem_mib):
    return pltpu.CompilerParams(dimension_semantics=sem, vmem_limit_bytes=vmem_mib * MIB)


def _seq_info(r0, lp, ls):
    in_prompt = r0 < lp
    idx = jnp.maximum(r0 - lp, 0) // ls
    start = jnp.where(in_prompt, 0, lp + idx * ls)
    length = jnp.where(in_prompt, lp, ls)
    return start, length


def _rmsnorm(x, g):
    ms = jnp.mean(x * x, axis=-1, keepdims=True)
    return x * lax.rsqrt(ms + EPS) * g


def _ffn_kernel(*refs, aliased, final_norm):
    if aliased:
        refs = refs[1:]
    x_ref, g_ref, wg_ref, wu_ref, wo_ref = refs[:5]
    o_ref, h_ref = refs[-2:]
    j = pl.program_id(1)

    @pl.when(j == 0)
    def _():
        x = x_ref[...]
        h_ref[...] = _rmsnorm(x, g_ref[...]).astype(BF16)
        o_ref[...] = x

    h = h_ref[...]
    gate = jnp.dot(h, wg_ref[...], preferred_element_type=F32)
    up = jnp.dot(h, wu_ref[...], preferred_element_type=F32)
    act = (gate * jax.nn.sigmoid(gate) * (0.5 * up)).astype(BF16)
    o_ref[...] += jnp.dot(act, wo_ref[...], preferred_element_type=F32)

    if final_norm:
        @pl.when(j == pl.num_programs(1) - 1)
        def _():
            o_ref[...] = _rmsnorm(o_ref[...], refs[5][...])


def _ffn(x, g, wi, wo, layer, tm, tf, *, in_row0=0, rows=None, out_rows=None, out_row0=0, base=None,
         final_g=None):
    d = x.shape[1]
    rows = x.shape[0] if rows is None else rows
    out_rows = rows if out_rows is None else out_rows
    dff = wo.shape[1]
    nj = dff // tf
    i0, o0 = in_row0 // tm, out_row0 // tm
    row_vec = pl.BlockSpec((1, d), lambda i, j: (0, 0))
    in_specs = [
        pl.BlockSpec((tm, d), lambda i, j: (i0 + i, 0)),
        row_vec,
        pl.BlockSpec((None, d, tf), lambda i, j: (layer, 0, j)),
        pl.BlockSpec((None, d, tf), lambda i, j: (layer, 0, j + nj)),
        pl.BlockSpec((None, tf, d), lambda i, j: (layer, j, 0)),
    ]
    args = [x, g.reshape(1, d), wi, wi, wo]
    if final_g is not None:
        in_specs.append(row_vec)
        args.append(final_g.reshape(1, d))
    if base is not None:
        in_specs.insert(0, pl.BlockSpec(memory_space=pl.ANY))
        args.insert(0, base)
    return pl.pallas_call(
        functools.partial(_ffn_kernel, aliased=base is not None, final_norm=final_g is not None),
        grid=(rows // tm, nj),
        in_specs=in_specs,
        out_specs=pl.BlockSpec((tm, d), lambda i, j: (o0 + i, 0)),
        out_shape=jax.ShapeDtypeStruct((out_rows, d), F32),
        input_output_aliases={} if base is None else {0: 0},
        scratch_shapes=[pltpu.VMEM((tm, d), BF16)],
        compiler_params=_params(("parallel", "arbitrary"), 56),
    )(*args)


def _norm_matmul_kernel(x_ref, g_ref, w_ref, o_ref, h_ref):
    @pl.when(pl.program_id(1) == 0)
    def _():
        h_ref[...] = _rmsnorm(x_ref[...], g_ref[...]).astype(BF16)

    o_ref[...] = jnp.dot(h_ref[...], w_ref[...], preferred_element_type=F32).astype(o_ref.dtype)


def _norm_matmul(x, g, w, layer, tm, tn):
    t, d = x.shape
    n = w.shape[2]
    return pl.pallas_call(
        _norm_matmul_kernel,
        grid=(t // tm, n // tn),
        in_specs=[
            pl.BlockSpec((tm, d), lambda i, j: (i, 0)),
            pl.BlockSpec((1, d), lambda i, j: (0, 0)),
            pl.BlockSpec((None, d, tn), lambda i, j: (layer, 0, j)),
        ],
        out_specs=pl.BlockSpec((tm, tn), lambda i, j: (i, j)),
        out_shape=jax.ShapeDtypeStruct((t, n), BF16),
        scratch_shapes=[pltpu.VMEM((tm, d), BF16)],
        compiler_params=_params(("parallel", "arbitrary"), 48),
    )(x, g.reshape(1, d), w)


def _proj_residual_kernel(*refs):
    x_ref, o_ref = refs[0], refs[-1]
    pairs = refs[1:-1]
    acc = x_ref[...]
    for k in range(len(pairs) // 2):
        acc = acc + jnp.dot(pairs[2 * k][...], pairs[2 * k + 1][...], preferred_element_type=F32)
    o_ref[...] = acc


def _proj_residual(x, ys, w, layer, tm):
    t, d = x.shape
    kw = ys[0].shape[1]
    in_specs = [pl.BlockSpec((tm, d), lambda i: (i, 0))]
    args = [x]
    for k, y in enumerate(ys):
        in_specs.append(pl.BlockSpec((tm, kw), lambda i: (i, 0)))
        in_specs.append(pl.BlockSpec((None, kw, d), lambda i, k=k: (layer, k, 0)))
        args += [y, w]
    return pl.pallas_call(
        _proj_residual_kernel,
        grid=(t // tm,),
        in_specs=in_specs,
        out_specs=pl.BlockSpec((tm, d), lambda i: (i, 0)),
        out_shape=jax.ShapeDtypeStruct((t, d), F32),
        compiler_params=_params(("parallel",), 48),
    )(*args)


def _pool_kernel(um_ref, up_ref, un_ref, w_ref, sc_ref, o_ref, ext_ref, *, tt, lp, ls):
    r0 = pl.program_id(0) * tt
    start, length = _seq_info(r0, lp, ls)
    first = r0 == start
    last = r0 + tt == start + length
    ext_ref[0:HALO, :] = jnp.where(first, 0.0, up_ref[...].astype(F32))
    ext_ref[HALO:HALO + tt, :] = um_ref[...].astype(F32)
    ext_ref[HALO + tt:, :] = jnp.where(last, 0.0, un_ref[...].astype(F32))
    pos = (r0 - start) + lax.broadcasted_iota(jnp.int32, (tt, 1), 0)
    for g, w in enumerate(POOL_WINDOWS):
        cs = slice(g * POOL_GROUP, (g + 1) * POOL_GROUP)
        s = ext_ref[pl.ds(HALO - w // 2, tt), cs]
        for off in range(-w // 2 + 1, w // 2):
            s = s + ext_ref[pl.ds(HALO + off, tt), cs]
        cnt = (jnp.minimum(pos + w // 2, length) - jnp.maximum(pos - w // 2, 0)).astype(F32)
        p = s / cnt - ext_ref[pl.ds(HALO, tt), cs]
        y = jnp.dot(p.astype(BF16), w_ref[g], preferred_element_type=F32) * sc_ref[:, cs]
        o_ref[:, cs] = y.astype(BF16)


def _pool(u, pool_w, scale, tt, lp, ls):
    t = u.shape[0]
    nh = t // HALO
    hb = tt // HALO
    return pl.pallas_call(
        functools.partial(_pool_kernel, tt=tt, lp=lp, ls=ls),
        grid=(t // tt,),
        in_specs=[
            pl.BlockSpec((tt, POOL_WIDTH), lambda i: (i, 0)),
            pl.BlockSpec((HALO, POOL_WIDTH), lambda i: (jnp.maximum(i * hb - 1, 0), 0)),
            pl.BlockSpec((HALO, POOL_WIDTH), lambda i: (jnp.minimum((i + 1) * hb, nh - 1), 0)),
            pl.BlockSpec((len(POOL_WINDOWS), POOL_GROUP, POOL_GROUP), lambda i: (0, 0, 0)),
            pl.BlockSpec((1, POOL_WIDTH), lambda i: (0, 0)),
        ],
        out_specs=pl.BlockSpec((tt, POOL_WIDTH), lambda i: (i, 0)),
        out_shape=jax.ShapeDtypeStruct((t, POOL_WIDTH), BF16),
        scratch_shapes=[pltpu.VMEM((tt + 2 * HALO, POOL_WIDTH), F32)],
        compiler_params=_params(("parallel",), 32),
    )(u, u, u, pool_w, scale.reshape(1, POOL_WIDTH))


def _gate_kernel(m0, m1, m2, p0, p1, p2, n0, n1, n2, cw_ref, cb_ref,
                 vvb_ref, vvf_ref, x0_ref, nyq_ref, *, tt, lp, ls):
    r0 = pl.program_id(0) * tt
    start, length = _seq_info(r0, lp, ls)
    first = r0 == start
    last = r0 + tt == start + length
    row = lax.broadcasted_iota(jnp.int32, (tt, 1), 0)

    def conv(m_ref, p_ref, n_ref, q):
        cs = slice(q * HYENA_WIDTH, (q + 1) * HYENA_WIDTH)
        m = m_ref[...].astype(F32)
        before = jnp.where(first, 0.0, p_ref[...].astype(F32)[HALO - 1:HALO, :])
        after = jnp.where(last, 0.0, n_ref[...].astype(F32)[0:1, :])
        up = jnp.where(row == 0, before, pltpu.roll(m, 1, 0))
        dn = jnp.where(row == tt - 1, after, pltpu.roll(m, tt - 1, 0))
        return up * cw_ref[0:1, cs] + m * cw_ref[1:2, cs] + dn * cw_ref[2:3, cs] + cb_ref[:, cs]

    x0 = conv(m0, p0, n0, 0)
    x1 = conv(m1, p1, n1, 1)
    v = conv(m2, p2, n2, 2)
    vv = v * x1
    vvb_ref[...] = vv.astype(BF16)
    vvf_ref[...] = vv
    x0_ref[...] = x0
    sign = jnp.where((row & 1) == 0, 1.0, -1.0)
    part = (vv * sign).reshape(tt // 8, 8, HYENA_WIDTH).sum(axis=0)

    @pl.when(first)
    def _():
        nyq_ref[...] = part

    @pl.when(jnp.logical_not(first))
    def _():
        nyq_ref[...] += part


def _gate(u, conv_w, conv_b, tt, lp, ls, n_seq):
    t = u.shape[0]
    nh = t // HALO
    hb = tt // HALO
    c = HYENA_WIDTH

    def seq_of(i):
        r0 = i * tt
        return jnp.where(r0 < lp, 0, 1 + jnp.maximum(r0 - lp, 0) // ls)

    main = [pl.BlockSpec((tt, c), lambda i, q=q: (i, 1 + q)) for q in range(3)]
    prev = [pl.BlockSpec((HALO, c), lambda i, q=q: (jnp.maximum(i * hb - 1, 0), 1 + q)) for q in range(3)]
    nxt = [pl.BlockSpec((HALO, c), lambda i, q=q: (jnp.minimum((i + 1) * hb, nh - 1), 1 + q)) for q in range(3)]
    return pl.pallas_call(
        functools.partial(_gate_kernel, tt=tt, lp=lp, ls=ls),
        grid=(t // tt,),
        in_specs=main + prev + nxt + [
            pl.BlockSpec((3, 3 * c), lambda i: (0, 0)),
            pl.BlockSpec((1, 3 * c), lambda i: (0, 0)),
        ],
        out_specs=[
            pl.BlockSpec((tt, c), lambda i: (i, 0)),
            pl.BlockSpec((tt, c), lambda i: (i, 0)),
            pl.BlockSpec((tt, c), lambda i: (i, 0)),
            pl.BlockSpec((None, 8, c), lambda i: (seq_of(i), 0, 0)),
        ],
        out_shape=[
            jax.ShapeDtypeStruct((t, c), BF16),
            jax.ShapeDtypeStruct((t, c), F32),
            jax.ShapeDtypeStruct((t, c), F32),
            jax.ShapeDtypeStruct((n_seq, 8, c), F32),
        ],
        compiler_params=_params(("arbitrary",), 48),
    )(*([u] * 9), conv_w, conv_b.reshape(1, 3 * c))


def _filter_kernel(z_ref, w1, b1, w2, b2, w3, b3, w4h, w4l, b4, fr_ref, dl_ref, e_ref, o_ref, st_ref,
                   *, tr, slab_major):
    i = pl.program_id(0)
    z = z_ref[...]
    fr = fr_ref[...]

    def dot(a, b):
        return jnp.dot(a, b, preferred_element_type=F32, precision=HIGHEST)

    def dot_bf16(a, b):
        return jnp.dot(a, b, preferred_element_type=F32)

    h = jnp.sin(fr * (dot(z, w1[...]) + b1[...]))
    h = jnp.sin(fr * (dot(h, w2[...]) + b2[...]))
    h = jnp.sin(fr * (dot(h, w3[...]) + b3[...]))
    hh_ = h.astype(BF16)
    hl_ = (h - hh_.astype(F32)).astype(BF16)
    hh = (dot_bf16(hl_, w4h[...]) + dot_bf16(hh_, w4l[...])) + dot_bf16(hh_, w4h[...]) + b4[...]
    decay = jnp.exp(-z[:, 0:1] * jnp.abs(dl_ref[...]))
    row = i * tr + lax.broadcasted_iota(jnp.int32, (tr, 1), 0)
    hf = hh[:, :HYENA_WIDTH] * decay
    hb = jnp.where(row == 0, 0.0, hh[:, HYENA_WIDTH:] * decay)
    e = hf + hb
    o = hf - hb
    if slab_major:
        for a in range(tr // DFT_INNER):
            e_ref[:, a, :] = e[a * DFT_INNER:(a + 1) * DFT_INNER].astype(e_ref.dtype)
            o_ref[:, a, :] = o[a * DFT_INNER:(a + 1) * DFT_INNER].astype(o_ref.dtype)
    else:
        e_ref[...] = e.astype(e_ref.dtype)
        o_ref[...] = o.astype(o_ref.dtype)
    sign = jnp.where((row & 1) == 0, 1.0, -1.0)
    norm_part = (jnp.abs(hf) + jnp.abs(hb)).reshape(tr // 8, 8, HYENA_WIDTH).sum(axis=0)
    nyq_part = (e * sign).reshape(tr // 8, 8, HYENA_WIDTH).sum(axis=0)

    @pl.when(i == 0)
    def _():
        st_ref[0] = norm_part
        st_ref[1] = nyq_part

    @pl.when(i > 0)
    def _():
        st_ref[0] += norm_part
        st_ref[1] += nyq_part


def _hyena_filter(length, fp, tr, slab_major):
    c = HYENA_WIDTH
    if slab_major:
        taps_spec = pl.BlockSpec((DFT_INNER, tr // DFT_INNER, c), lambda i: (0, i, 0))
        taps_shape = jax.ShapeDtypeStruct((DFT_INNER, length // DFT_INNER, c), F32)
    else:
        taps_spec = pl.BlockSpec((tr, c), lambda i: (i, 0))
        taps_shape = jax.ShapeDtypeStruct((length, c), BF16)
    t = jnp.linspace(0.0, 1.0, length, dtype=F32)[:, None]
    w = 2.0 * math.pi * jnp.arange(length, dtype=F32)[:, None] / length
    f = jnp.linspace(1e-4, HYENA_EMB_BANDS - 1, HYENA_EMB_BANDS, dtype=F32)[None, :]
    z = jnp.concatenate([t, jnp.cos(f * w), -jnp.sin(f * w)], axis=-1)
    z = jnp.pad(z, ((0, 0), (0, 128 - z.shape[1])))
    min_decay = math.log(HYENA_TARGET) / HYENA_SLOW_DECAY
    max_decay = math.log(HYENA_TARGET) / HYENA_FAST_DECAY
    deltas = jnp.linspace(min_decay, max_decay, c, dtype=F32)[None, :]
    full = lambda shape: pl.BlockSpec(shape, lambda i: (0,) * len(shape))
    return pl.pallas_call(
        functools.partial(_filter_kernel, tr=tr, slab_major=slab_major),
        grid=(length // tr,),
        in_specs=[pl.BlockSpec((tr, 128), lambda i: (i, 0))] + [full(a.shape) for a in fp] + [full((1, c))],
        out_specs=[taps_spec, taps_spec, full((2, 8, c))],
        out_shape=[taps_shape, taps_shape, jax.ShapeDtypeStruct((2, 8, c), F32)],
        compiler_params=_params(("arbitrary",), 56),
    )(z, *fp, deltas)


def _pad_filter_params(w1, b1, w2, b2, w3, b3, w4, b4, freq):
    hid = 128
    pad2 = lambda a, r, c: jnp.pad(a.astype(F32), ((0, r - a.shape[0]), (0, c - a.shape[1])))
    row = lambda a: jnp.pad(a.astype(F32), (0, hid - a.shape[0])).reshape(1, hid)
    w4p = pad2(w4, hid, w4.shape[1])
    w4h = w4p.astype(BF16)
    w4l = (w4p - w4h.astype(F32)).astype(BF16)
    return (pad2(w1, 128, hid), row(b1), pad2(w2, hid, hid), row(b2), pad2(w3, hid, hid), row(b3),
            w4h, w4l, b4.astype(F32).reshape(1, -1), row(freq))


def _dft_mats(length):
    n = 2 * length
    k = jnp.arange(length, dtype=jnp.int32)[:, None]

    def table(cols):
        ang = ((k * cols[None, :]) % n).astype(F32) * (2.0 * math.pi / n)
        return jnp.cos(ang), jnp.sin(ang)

    ca, sa = table(jnp.arange(length // 128, dtype=jnp.int32) * 128)
    cb, sb = table(jnp.arange(128, dtype=jnp.int32))
    cos = ca[:, :, None] * cb[:, None, :] - sa[:, :, None] * sb[:, None, :]
    sin = sa[:, :, None] * cb[:, None, :] + ca[:, :, None] * sb[:, None, :]
    return cos.reshape(length, length).astype(BF16), sin.reshape(length, length).astype(BF16)


def _spectrum_scale(st_ref, k0, tm, length):
    norm = jnp.sum(st_ref[0], axis=0, keepdims=True)
    krow = k0 + lax.broadcasted_iota(jnp.int32, (tm, 1), 0)
    wk = jnp.where(krow == 0, 1.0, 2.0)
    return wk / (2.0 * length * norm)


def _dft_filter_kernel(c_ref, s_ref, e_ref, o_ref, st_ref, kr_ref, ki_ref, accr, acci, *, tm, length):
    kt = pl.program_id(1)

    @pl.when(kt == 0)
    def _():
        accr[...] = jnp.zeros_like(accr)
        acci[...] = jnp.zeros_like(acci)

    accr[...] += jnp.dot(c_ref[...], e_ref[...], preferred_element_type=F32)
    acci[...] += jnp.dot(s_ref[...], o_ref[...], preferred_element_type=F32)

    @pl.when(kt == pl.num_programs(1) - 1)
    def _():
        sc = _spectrum_scale(st_ref, pl.program_id(0) * tm, tm, length)
        kr_ref[...] = accr[...] * sc
        ki_ref[...] = -acci[...] * sc


def _dft_filter(cm, sm, e, o, st, tm, tk):
    length = cm.shape[0]
    c = HYENA_WIDTH
    return pl.pallas_call(
        functools.partial(_dft_filter_kernel, tm=tm, length=length),
        grid=(length // tm, length // tk),
        in_specs=[
            pl.BlockSpec((tm, tk), lambda m, k: (m, k)),
            pl.BlockSpec((tm, tk), lambda m, k: (m, k)),
            pl.BlockSpec((tk, c), lambda m, k: (k, 0)),
            pl.BlockSpec((tk, c), lambda m, k: (k, 0)),
            pl.BlockSpec((2, 8, c), lambda m, k: (0, 0, 0)),
        ],
        out_specs=[pl.BlockSpec((tm, c), lambda m, k: (m, 0))] * 2,
        out_shape=[jax.ShapeDtypeStruct((length, c), F32)] * 2,
        scratch_shapes=[pltpu.VMEM((tm, c), F32)] * 2,
        compiler_params=_params(("parallel", "arbitrary"), 48),
    )(cm, sm, e, o, st)


def _dft_fwd_kernel(c_ref, s_ref, v_ref, kr_ref, ki_ref, yr_ref, yi_ref, accr, acci):
    kt = pl.program_id(2)

    @pl.when(kt == 0)
    def _():
        accr[...] = jnp.zeros_like(accr)
        acci[...] = jnp.zeros_like(acci)

    v = v_ref[...]
    accr[...] += jnp.dot(c_ref[...], v, preferred_element_type=F32)
    acci[...] += jnp.dot(s_ref[...], v, preferred_element_type=F32)

    @pl.when(kt == pl.num_programs(2) - 1)
    def _():
        vr = accr[...]
        vi = -acci[...]
        kr = kr_ref[...]
        ki = ki_ref[...]
        yr_ref[...] = (vr * kr - vi * ki).astype(BF16)
        yi_ref[...] = (vr * ki + vi * kr).astype(BF16)


def _dft_fwd(cm, sm, vvb, kr, ki, row0, batch, tm, tk):
    length = cm.shape[0]
    c = HYENA_WIDTH
    nm, nk = length // tm, length // tk
    off_k = row0 // tk
    return pl.pallas_call(
        _dft_fwd_kernel,
        grid=(nm, batch, nk),
        in_specs=[
            pl.BlockSpec((tm, tk), lambda m, b, k: (m, k)),
            pl.BlockSpec((tm, tk), lambda m, b, k: (m, k)),
            pl.BlockSpec((tk, c), lambda m, b, k: (off_k + b * nk + k, 0)),
            pl.BlockSpec((tm, c), lambda m, b, k: (m, 0)),
            pl.BlockSpec((tm, c), lambda m, b, k: (m, 0)),
        ],
        out_specs=[pl.BlockSpec((tm, c), lambda m, b, k: (b * nm + m, 0))] * 2,
        out_shape=[jax.ShapeDtypeStruct((batch * length, c), BF16)] * 2,
        scratch_shapes=[pltpu.VMEM((tm, c), F32)] * 2,
        compiler_params=_params(("parallel", "parallel", "arbitrary"), 48),
    )(cm, sm, vvb, kr, ki)


def _dft_inv_kernel(c_ref, s_ref, yr_ref, yi_ref, st_ref, vn_ref, vv_ref, x0_ref, d_ref, o_ref, acc,
                    *, tm, length):
    kt = pl.program_id(2)

    @pl.when(kt == 0)
    def _():
        acc[...] = jnp.zeros_like(acc)

    acc[...] += (jnp.dot(c_ref[...], yr_ref[...], preferred_element_type=F32)
                 - jnp.dot(s_ref[...], yi_ref[...], preferred_element_type=F32))

    @pl.when(kt == pl.num_programs(2) - 1)
    def _():
        norm = jnp.sum(st_ref[0], axis=0, keepdims=True)
        knyq = jnp.sum(st_ref[1], axis=0, keepdims=True)
        vnyq = jnp.sum(vn_ref[...], axis=0, keepdims=True)
        ynyq = vnyq * knyq / (2.0 * length * norm)
        row = lax.broadcasted_iota(jnp.int32, (tm, 1), 0)
        sign = jnp.where((row & 1) == 0, 1.0, -1.0)
        y = acc[...] + sign * ynyq
        o_ref[...] = ((y + vv_ref[...] * d_ref[...]) * x0_ref[...]).astype(BF16)


def _dft_inv_aliased_kernel(base_ref, *refs, tm, length):
    del base_ref
    _dft_inv_kernel(*refs, tm=tm, length=length)


def _dft_inv(cm, sm, yr, yi, st, vnyq, vvf, x0, d, row0, batch, tm, tk, base=None):
    length = cm.shape[0]
    c = HYENA_WIDTH
    nm, nk = length // tm, length // tk
    off_m = row0 // tm
    seq0 = 0 if row0 == 0 else 1
    body = _dft_inv_kernel if base is None else _dft_inv_aliased_kernel
    lead_specs = [] if base is None else [pl.BlockSpec(memory_space=pl.ANY)]
    lead_args = [] if base is None else [base]
    return pl.pallas_call(
        functools.partial(body, tm=tm, length=length),
        grid=(nm, batch, nk),
        input_output_aliases={} if base is None else {0: 0},
        in_specs=lead_specs + [
            pl.BlockSpec((tm, tk), lambda m, b, k: (m, k)),
            pl.BlockSpec((tm, tk), lambda m, b, k: (m, k)),
            pl.BlockSpec((tk, c), lambda m, b, k: (b * nk + k, 0)),
            pl.BlockSpec((tk, c), lambda m, b, k: (b * nk + k, 0)),
            pl.BlockSpec((2, 8, c), lambda m, b, k: (0, 0, 0)),
            pl.BlockSpec((None, 8, c), lambda m, b, k: (seq0 + b, 0, 0)),
            pl.BlockSpec((tm, c), lambda m, b, k: (off_m + b * nm + m, 0)),
            pl.BlockSpec((tm, c), lambda m, b, k: (off_m + b * nm + m, 0)),
            pl.BlockSpec((1, c), lambda m, b, k: (0, 0)),
        ],
        out_specs=pl.BlockSpec((tm, c), lambda m, b, k: (off_m + b * nm + m, 0)),
        out_shape=jax.ShapeDtypeStruct((vvf.shape[0], c), BF16),
        scratch_shapes=[pltpu.VMEM((tm, c), F32)],
        compiler_params=_params(("parallel", "parallel", "arbitrary"), 48),
    )(*lead_args, cm, sm, yr, yi, st, vnyq, vvf, x0, d.reshape(1, c))


def _pack_pair(re, im):
    hi = lax.bitcast_convert_type(re.astype(BF16).astype(F32), jnp.uint32)
    lo = lax.bitcast_convert_type(im.astype(BF16).astype(F32), jnp.uint32)
    return (hi & jnp.uint32(0xFFFF0000)) | (lo >> 16)


def _unpack_stacked(p):
    re = lax.bitcast_convert_type(p & jnp.uint32(0xFFFF0000), F32)
    im = lax.bitcast_convert_type(p << 16, F32)
    return jnp.concatenate([re, im], axis=0).astype(BF16)


def _two_stage_tables(length):
    n = 2 * length
    n1 = n // DFT_INNER
    a = jnp.arange(n1, dtype=jnp.int32)
    ang1 = ((a[:, None] * a[None, :n1 // 2]) % n1).astype(F32) * (2.0 * math.pi / n1)
    c1, s1 = jnp.cos(ang1), jnp.sin(ang1)
    f1 = jnp.concatenate([c1, -s1], axis=0).astype(BF16)
    f3 = jnp.concatenate([c1.T, -s1.T], axis=1).astype(BF16)
    b = jnp.arange(DFT_INNER, dtype=jnp.int32)
    k = a[:, None, None] + n1 * b[None, :, None]
    ang2 = ((k * b[None, None, :]) % n).astype(F32) * (2.0 * math.pi / n)
    c2, s2 = jnp.cos(ang2), jnp.sin(ang2)
    m1 = jnp.concatenate([jnp.concatenate([c2, s2], axis=2), jnp.concatenate([-s2, c2], axis=2)], axis=1)
    return f1, f3, m1.astype(BF16), jnp.swapaxes(m1, 1, 2).astype(BF16)


def _s1_kernel(f_ref, x_ref, a_ref, *, tn, slab_major):
    n1 = f_ref.shape[0] // 2
    for j in range(tn):
        xj = (x_ref[j] if slab_major else x_ref[:, j, :]).astype(BF16)
        a = jnp.dot(f_ref[...], xj, preferred_element_type=F32)
        a_ref[:, j, :] = _pack_pair(a[:n1], a[n1:])


def _stage1(f1, x3, tn, slab_major):
    rows2, half = f1.shape
    c = x3.shape[2]
    x_spec = (pl.BlockSpec((tn, half, c), lambda i: (i, 0, 0)) if slab_major
              else pl.BlockSpec((half, tn, c), lambda i: (0, i, 0)))
    return pl.pallas_call(
        functools.partial(_s1_kernel, tn=tn, slab_major=slab_major),
        grid=(DFT_INNER // tn,),
        in_specs=[pl.BlockSpec((rows2, half), lambda i: (0, 0)), x_spec],
        out_specs=pl.BlockSpec((rows2 // 2, tn, c), lambda i: (0, i, 0)),
        out_shape=jax.ShapeDtypeStruct((rows2 // 2, DFT_INNER, c), jnp.uint32),
        compiler_params=_params(("parallel",), 40),
    )(f1, x3)


def _mid_filter_kernel(e_ref, o_ref, m1_ref, st_ref, k_ref, *, kb, length):
    scale = 1.0 / (2.0 * length * jnp.sum(st_ref[0], axis=0, keepdims=True))
    for q in range(kb):
        xe = jnp.dot(m1_ref[q], _unpack_stacked(e_ref[q]), preferred_element_type=F32)
        xo = jnp.dot(m1_ref[q], _unpack_stacked(o_ref[q]), preferred_element_type=F32)
        k_ref[0, q] = xe[:DFT_INNER] * scale
        k_ref[1, q] = xo[DFT_INNER:] * scale


def _mid_filter(ae, ao, m1, st, kb, ct):
    n1 = m1.shape[0]
    c = HYENA_WIDTH
    blk = pl.BlockSpec((kb, DFT_INNER, ct), lambda i, j: (i, 0, j))
    return pl.pallas_call(
        functools.partial(_mid_filter_kernel, kb=kb, length=n1 * DFT_INNER // 2),
        grid=(n1 // kb, c // ct),
        in_specs=[blk, blk,
                  pl.BlockSpec((kb, 2 * DFT_INNER, 2 * DFT_INNER), lambda i, j: (i, 0, 0)),
                  pl.BlockSpec((2, 8, ct), lambda i, j: (0, 0, j))],
        out_specs=pl.BlockSpec((2, kb, DFT_INNER, ct), lambda i, j: (0, i, 0, j)),
        out_shape=jax.ShapeDtypeStruct((2, n1, DFT_INNER, c), F32),
        compiler_params=_params(("parallel", "parallel"), 40),
    )(ae, ao, m1, st)


def _mid_kernel(a_ref, m1_ref, m2_ref, k_ref, z_ref, *, kb):
    for q in range(kb):
        x = jnp.dot(m1_ref[q], _unpack_stacked(a_ref[q]), preferred_element_type=F32)
        xr, xi = x[:DFT_INNER], x[DFT_INNER:]
        kr, ki = k_ref[0, q], k_ref[1, q]
        y = jnp.concatenate([xr * kr - xi * ki, xr * ki + xi * kr], axis=0).astype(BF16)
        z = jnp.dot(m2_ref[q], y, preferred_element_type=F32)
        z_ref[:, q, :] = _pack_pair(z[:DFT_INNER], z[DFT_INNER:])


def _mid(a, m1, m2, kf, kb, ct):
    n1 = m1.shape[0]
    c = HYENA_WIDTH
    mat = pl.BlockSpec((kb, 2 * DFT_INNER, 2 * DFT_INNER), lambda i, j: (i, 0, 0))
    return pl.pallas_call(
        functools.partial(_mid_kernel, kb=kb),
        grid=(n1 // kb, c // ct),
        in_specs=[pl.BlockSpec((kb, DFT_INNER, ct), lambda i, j: (i, 0, j)), mat, mat,
                  pl.BlockSpec((2, kb, DFT_INNER, ct), lambda i, j: (0, i, 0, j))],
        out_specs=pl.BlockSpec((DFT_INNER, kb, ct), lambda i, j: (0, i, j)),
        out_shape=jax.ShapeDtypeStruct((DFT_INNER, n1, c), jnp.uint32),
        compiler_params=_params(("parallel", "parallel"), 40),
    )(a, m1, m2, kf)


def _s3_kernel(f_ref, z_ref, y_ref, *, tn):
    for j in range(tn):
        y_ref[:, j, :] = jnp.dot(f_ref[...], _unpack_stacked(z_ref[j]), preferred_element_type=F32)


def _stage3(f3, z, tn):
    half = f3.shape[0]
    n1, c = z.shape[1], z.shape[2]
    return pl.pallas_call(
        functools.partial(_s3_kernel, tn=tn),
        grid=(DFT_INNER // tn,),
        in_specs=[pl.BlockSpec((half, 2 * n1), lambda i: (0, 0)), pl.BlockSpec((tn, n1, c), lambda i: (i, 0, 0))],
        out_specs=pl.BlockSpec((half, tn, c), lambda i: (0, i, 0)),
        out_shape=jax.ShapeDtypeStruct((half, DFT_INNER, c), F32),
        compiler_params=_params(("parallel",), 40),
    )(f3, z)


def _hy_out_kernel(y_ref, vv_ref, x0_ref, d_ref, o_ref):
    o_ref[...] = ((y_ref[...] + vv_ref[...] * d_ref[...]) * x0_ref[...]).astype(BF16)


def _hy_out(y, vvf, x0, d, tt):
    c = HYENA_WIDTH
    row = pl.BlockSpec((tt, c), lambda i: (i, 0))
    return pl.pallas_call(
        _hy_out_kernel,
        grid=(y.shape[0] // tt,),
        in_specs=[row, row, row, pl.BlockSpec((1, c), lambda i: (0, 0))],
        out_specs=row,
        out_shape=jax.ShapeDtypeStruct((vvf.shape[0], c), BF16),
        compiler_params=_params(("parallel",), 32),
    )(y, vvf, x0, d.reshape(1, c))


def _t5_buckets(rel):
    nb = N_BUCKETS // 2
    max_exact = nb // 2
    ret = (rel > 0).astype(jnp.int32) * nb
    n = jnp.abs(rel)
    large = max_exact + (jnp.log(jnp.maximum(n, 1).astype(F32) / max_exact)
                         / math.log(MAX_DISTANCE / max_exact) * (nb - max_exact)).astype(jnp.int32)
    large = jnp.minimum(large, nb - 1)
    return ret + jnp.where(n < max_exact, n, large)


def _bias_kernel(bk_ref, rb_ref, o_ref):
    h = pl.program_id(0)
    bk = bk_ref[...]
    qi = lax.broadcasted_iota(jnp.int32, bk.shape, 0)
    pj = lax.broadcasted_iota(jnp.int32, bk.shape, 1) - BLOCK
    acc = jnp.zeros(bk.shape, F32)
    for b in range(N_BUCKETS):
        acc = jnp.where(bk == b, rb_ref[b, h], acc)
    o_ref[...] = jnp.where(jnp.abs(pj - qi) <= WINDOW, acc, MASKED)


def _attn_bias(rel_bias):
    qi = jnp.arange(BLOCK)[:, None]
    pj = jnp.arange(3 * BLOCK)[None, :] - BLOCK
    buckets = _t5_buckets(pj - qi).astype(jnp.int32)
    return pl.pallas_call(
        _bias_kernel,
        grid=(N_HEADS,),
        in_specs=[
            pl.BlockSpec((BLOCK, 3 * BLOCK), lambda h: (0, 0)),
            pl.BlockSpec(memory_space=pltpu.SMEM),
        ],
        out_specs=pl.BlockSpec((None, BLOCK, 3 * BLOCK), lambda h: (h, 0, 0)),
        out_shape=jax.ShapeDtypeStruct((N_HEADS, BLOCK, 3 * BLOCK), F32),
        compiler_params=_params(("arbitrary",), 16),
    )(buckets, rel_bias.astype(F32))


def _attn_kernel(q_ref, kp_ref, kc_ref, kn_ref, vp_ref, vc_ref, vn_ref, bias_ref, sink_ref, o_ref, *, lp, ls):
    r0 = pl.program_id(0) * BLOCK
    start, length = _seq_info(r0, lp, ls)
    first = r0 == start
    last = r0 + BLOCK == start + length
    rows = GQA_GROUP * BLOCK
    col = lax.broadcasted_iota(jnp.int32, (rows, 3 * BLOCK), 1)
    outside = (first & (col < BLOCK)) | (last & (col >= 2 * BLOCK))
    grp = lax.broadcasted_iota(jnp.int32, (rows, 1), 0) // BLOCK
    scale = 1.0 / math.sqrt(HEAD_DIM)
    for kh in range(N_KV_HEADS):
        hs = slice(kh * HEAD_DIM, (kh + 1) * HEAD_DIM)
        kcat = jnp.concatenate([kp_ref[:, hs], kc_ref[:, hs], kn_ref[:, hs]], axis=0)
        vcat = jnp.concatenate([vp_ref[:, hs], vc_ref[:, hs], vn_ref[:, hs]], axis=0)
        heads = [kh * GQA_GROUP + g for g in range(GQA_GROUP)]
        q4 = jnp.concatenate([q_ref[:, h * HEAD_DIM:(h + 1) * HEAD_DIM] for h in heads], axis=0)
        s = lax.dot_general(q4, kcat, (((1,), (1,)), ((), ())), preferred_element_type=F32) * scale
        s = jnp.where(outside, MASKED, s + bias_ref[kh])
        sink = jnp.zeros((rows, 1), F32)
        for g, h in enumerate(heads):
            sink = jnp.where(grp == g, sink_ref[h], sink)
        m = jnp.maximum(jnp.max(s, axis=-1, keepdims=True), sink)
        p = jnp.exp(s - m)
        denom = jnp.sum(p, axis=-1, keepdims=True) + jnp.exp(sink - m)
        o = jnp.dot(p.astype(BF16), vcat, preferred_element_type=F32) / denom
        for g, h in enumerate(heads):
            o_ref[:, h * HEAD_DIM:(h + 1) * HEAD_DIM] = o[g * BLOCK:(g + 1) * BLOCK].astype(BF16)


def _attention(qkv, bias, sink, lp, ls):
    t = qkv.shape[0]
    nb = t // BLOCK
    qw = N_HEADS * HEAD_DIM
    kvw = N_KV_HEADS * HEAD_DIM
    kcol = qw // kvw
    prev = lambda i: jnp.maximum(i - 1, 0)
    nxt = lambda i: jnp.minimum(i + 1, nb - 1)
    kv_specs = [pl.BlockSpec((BLOCK, kvw), lambda i, f=f, cb=cb: (f(i), cb))
                for cb in (kcol, kcol + 1) for f in (prev, lambda i: i, nxt)]
    return pl.pallas_call(
        functools.partial(_attn_kernel, lp=lp, ls=ls),
        grid=(nb,),
        in_specs=[pl.BlockSpec((BLOCK, qw), lambda i: (i, 0))] + kv_specs + [
            pl.BlockSpec((N_KV_HEADS, GQA_GROUP * BLOCK, 3 * BLOCK), lambda i: (0, 0, 0)),
            pl.BlockSpec(memory_space=pltpu.SMEM),
        ],
        out_specs=pl.BlockSpec((BLOCK, qw), lambda i: (i, 0)),
        out_shape=jax.ShapeDtypeStruct((t, qw), BF16),
        compiler_params=_params(("parallel",), 32),
    )(qkv, *([qkv] * 6), bias.reshape(N_KV_HEADS, GQA_GROUP * BLOCK, 3 * BLOCK), sink.astype(F32))


def _tiles(lp, ls, dff):
    tok = min(512, ls)
    return dict(
        tm=tok,
        fm=min(1024, ls),
        tf=512 if dff % 512 == 0 else dff,
        tn=min(1024, ls),
        tt=min(512, ls),
        dm=min(512, ls),
        dk=min(2048, ls),
        tn2=8,
        kb=8,
        ct=512,
    )


def _forward(x_prompt, x_sample, p):
    depth = p['norm_ffn1'].shape[0]
    lp = x_prompt.shape[0] * x_prompt.shape[1]
    assert x_prompt.shape[0] == 1
    bs, ls = x_sample.shape[0], x_sample.shape[1]
    n_seq = 1 + bs
    assert lp % ls == 0
    dff = p['ffn1_wo'].shape[1]
    tl = _tiles(lp, ls, dff)
    tm, tt = tl['tm'], tl['tt']

    total = lp + bs * ls
    bf = lambda a: a.astype(BF16)
    ffn1_wi, ffn1_wo = bf(p['ffn1_wi']), bf(p['ffn1_wo'])
    ffn2_wi, ffn2_wo = bf(p['ffn2_wi']), bf(p['ffn2_wo'])
    ab_w_in, ab_w_out, pool_w = bf(p['ab_w_in']), bf(p['ab_w_out']), bf(p['pool_w'])
    w_qkv, w_o = bf(p['attn_w_qkv']), bf(p['attn_w_o'])
    bias = _attn_bias(p['rel_bias'])
    segments = ((0, 1, lp), (lp, bs, ls))
    lengths = {lp, ls}
    tables = {length: _two_stage_tables(length) for length in lengths if length >= TWO_STAGE_MIN_LEN}
    mats = {length: _dft_mats(length) for length in lengths if length < TWO_STAGE_MIN_LEN}

    for layer in range(depth):
        ffn1 = functools.partial(_ffn, g=p['norm_ffn1'][layer], wi=ffn1_wi, wo=ffn1_wo, layer=layer,
                                 tm=tl['fm'], tf=tl['tf'])
        if layer == 0:
            x = ffn1(x_prompt.reshape(lp, D_MODEL), out_rows=total)
            x = ffn1(x_sample.reshape(bs * ls, D_MODEL), out_rows=total, out_row0=lp, base=x)
        else:
            x = ffn1(x)
        i = layer // 2
        if layer % 2 == 0:
            u = _norm_matmul(x, p['norm_mix'][layer], ab_w_in, i, tl['fm'], tl['tn'])
            ya = _pool(u, pool_w[i], p['pool_scale'][i], tt, lp, ls)
            vvb, vvf, x0, vnyq = _gate(u, p['hy_conv_w'][i].astype(F32), p['hy_conv_b'][i].astype(F32),
                                       tt, lp, ls, n_seq)
            fp = _pad_filter_params(p['hy_ff_w1'][i], p['hy_ff_b1'][i], p['hy_ff_w2'][i], p['hy_ff_b2'][i],
                                    p['hy_ff_w3'][i], p['hy_ff_b3'][i], p['hy_ff_w4'][i], p['hy_ff_b4'][i],
                                    p['hy_freq'][i])
            d = p['hy_d'][i].astype(F32)
            yb = None
            for row0, batch, length in segments:
                two_stage = length in tables
                e, o, st = _hyena_filter(length, fp, min(1024 if two_stage else 512, length), two_stage)
                if two_stage:
                    assert row0 == 0 and batch == 1 and yb is None
                    f1, f3, m1, m2 = tables[length]
                    tn, kb, ct = tl['tn2'], tl['kb'], tl['ct']
                    kf = _mid_filter(_stage1(f1, e, tn, True), _stage1(f1, o, tn, True), m1, st, kb, ct)
                    v3 = vvf.reshape(-1, DFT_INNER, HYENA_WIDTH)
                    z = _mid(_stage1(f1, v3, tn, False), m1, m2, kf, kb, ct)
                    y = _stage3(f3, z, tn).reshape(length, HYENA_WIDTH)
                    yb = _hy_out(y, vvf, x0, d, tt)
                else:
                    cm, sm = mats[length]
                    dm, dk = min(tl['dm'], length), min(tl['dk'], length)
                    kr, ki = _dft_filter(cm, sm, e, o, st, dm, dk)
                    yr, yi = _dft_fwd(cm, sm, vvb, kr, ki, row0, batch, dm, dk)
                    yb = _dft_inv(cm, sm, yr, yi, st, vnyq, vvf, x0, d, row0, batch, dm, dk, base=yb)
            x = _proj_residual(x, [ya, yb], ab_w_out, i, tm)
        else:
            qkv = _norm_matmul(x, p['norm_mix'][layer], w_qkv, i, tl['fm'], tl['tn'])
            o = _attention(qkv, bias, p['attn_sink'][i], lp, ls)
            x = _proj_residual(x, [o], w_o, i, tm)
        ffn2 = functools.partial(_ffn, x, p['norm_ffn2'][layer], ffn2_wi, ffn2_wo, layer, tl['fm'], tl['tf'])
        if layer < depth - 1:
            x = ffn2()
    y_prompt = ffn2(rows=lp, final_g=p['norm_final'])
    y_sample = ffn2(in_row0=lp, rows=bs * ls, final_g=p['norm_final'])
    return y_prompt.reshape(x_prompt.shape), y_sample.reshape(x_sample.shape)


def kernel(x_prompt, x_sample, norm_ffn1, ffn1_wi, ffn1_wo, norm_mix, ab_w_in, pool_w, pool_scale, hy_conv_w, hy_conv_b, hy_ff_w1, hy_ff_b1, hy_ff_w2, hy_ff_b2, hy_ff_w3, hy_ff_b3, hy_ff_w4, hy_ff_b4, hy_freq, hy_d, ab_w_out, attn_w_qkv, attn_w_o, attn_sink, rel_bias, norm_ffn2, ffn2_wi, ffn2_wo, norm_final):
    p = dict(norm_ffn1=norm_ffn1, ffn1_wi=ffn1_wi, ffn1_wo=ffn1_wo, norm_mix=norm_mix,
             ab_w_in=ab_w_in, pool_w=pool_w, pool_scale=pool_scale,
             hy_conv_w=hy_conv_w, hy_conv_b=hy_conv_b, hy_ff_w1=hy_ff_w1, hy_ff_b1=hy_ff_b1,
             hy_ff_w2=hy_ff_w2, hy_ff_b2=hy_ff_b2, hy_ff_w3=hy_ff_w3, hy_ff_b3=hy_ff_b3,
             hy_ff_w4=hy_ff_w4, hy_ff_b4=hy_ff_b4, hy_freq=hy_freq, hy_d=hy_d, ab_w_out=ab_w_out,
             attn_w_qkv=attn_w_qkv, attn_w_o=attn_w_o, attn_sink=attn_sink, rel_bias=rel_bias,
             norm_ffn2=norm_ffn2, ffn2_wi=ffn2_wi, ffn2_wo=ffn2_wo, norm_final=norm_final)
    return _forward(x_prompt, x_sample, p)
```

```python
import functools
import math

import jax
import jax.numpy as jnp
import numpy as np
from jax import lax
from jax.experimental import pallas as pl
from jax.experimental.pallas import tpu as pltpu

F32 = jnp.float32
BF16 = jnp.bfloat16
HIGHEST = lax.Precision.HIGHEST

D_MODEL = 2048
EPS = 1e-6
POOL_WIDTH = 1024
POOL_WINDOWS = (2, 4, 8, 16)
POOL_GROUP = 256
HYENA_WIDTH = 1024
HYENA_EMB_BANDS = 16
HYENA_FAST_DECAY = 0.3
HYENA_SLOW_DECAY = 1.5
HYENA_TARGET = 1e-2
N_HEADS = 16
N_KV_HEADS = 4
HEAD_DIM = 128
GQA_GROUP = 4
WINDOW = 128
BLOCK = 128
N_BUCKETS = 32
MAX_DISTANCE = 128
MASKED = -1e30
HALO = 16
MIB = 1024 * 1024
FILTER_LANES = 64
DFT_INNER = 128
TWO_STAGE_MIN_LEN = 4096


def _params(sem, vmem_mib):
    return pltpu.CompilerParams(dimension_semantics=sem, vmem_limit_bytes=vmem_mib * MIB)


def _seq_info(r0, lp, ls):
    in_prompt = r0 < lp
    idx = jnp.maximum(r0 - lp, 0) // ls
    start = jnp.where(in_prompt, 0, lp + idx * ls)
    length = jnp.where(in_prompt, lp, ls)
    return start, length


def _rmsnorm(x, g):
    ms = jnp.mean(x * x, axis=-1, keepdims=True)
    return x * lax.rsqrt(ms + EPS) * g


def _ffn_kernel(*refs, aliased, final_norm):
    if aliased:
        refs = refs[1:]
    x_ref, g_ref, wg_ref, wu_ref, wo_ref = refs[:5]
    o_ref, h_ref = refs[-2:]
    j = pl.program_id(1)

    def branch(h):
        gate = jnp.dot(h, wg_ref[...], preferred_element_type=F32)
        up = jnp.dot(h, wu_ref[...], preferred_element_type=F32)
        act = (gate * jax.nn.sigmoid(gate) * (0.5 * up)).astype(BF16)
        return jnp.dot(act, wo_ref[...], preferred_element_type=F32)

    @pl.when(j == 0)
    def _():
        h = _rmsnorm(x_ref[...], g_ref[...]).astype(BF16)
        h_ref[...] = h
        o_ref[...] = x_ref[...] + branch(h)

    @pl.when(j > 0)
    def _():
        o_ref[...] += branch(h_ref[...])

    if final_norm:
        @pl.when(j == pl.num_programs(1) - 1)
        def _():
            o_ref[...] = _rmsnorm(o_ref[...], refs[5][...])


def _ffn(x, g, wi, wo, layer, tm, tf, *, in_row0=0, rows=None, out_rows=None, out_row0=0, base=None,
         final_g=None):
    d = x.shape[1]
    rows = x.shape[0] if rows is None else rows
    out_rows = rows if out_rows is None else out_rows
    dff = wo.shape[1]
    nj = dff // tf
    i0, o0 = in_row0 // tm, out_row0 // tm
    row_vec = pl.BlockSpec((1, d), lambda i, j: (0, 0))
    in_specs = [
        pl.BlockSpec((tm, d), lambda i, j: (i0 + i, 0)),
        row_vec,
        pl.BlockSpec((None, d, tf), lambda i, j: (layer, 0, j)),
        pl.BlockSpec((None, d, tf), lambda i, j: (layer, 0, j + nj)),
        pl.BlockSpec((None, tf, d), lambda i, j: (layer, j, 0)),
    ]
    args = [x, g.reshape(1, d), wi, wi, wo]
    if final_g is not None:
        in_specs.append(row_vec)
        args.append(final_g.reshape(1, d))
    if base is not None:
        in_specs.insert(0, pl.BlockSpec(memory_space=pl.ANY))
        args.insert(0, base)
    return pl.pallas_call(
        functools.partial(_ffn_kernel, aliased=base is not None, final_norm=final_g is not None),
        grid=(rows // tm, nj),
        in_specs=in_specs,
        out_specs=pl.BlockSpec((tm, d), lambda i, j: (o0 + i, 0)),
        out_shape=jax.ShapeDtypeStruct((out_rows, d), F32),
        input_output_aliases={} if base is None else {0: 0},
        scratch_shapes=[pltpu.VMEM((tm, d), BF16)],
        compiler_params=_params(("parallel", "arbitrary"), 56),
    )(*args)


def _norm_matmul_kernel(x_ref, g_ref, w_ref, o_ref, h_ref):
    j = pl.program_id(1)

    @pl.when(j == 0)
    def _():
        h = _rmsnorm(x_ref[...], g_ref[...]).astype(BF16)
        h_ref[...] = h
        o_ref[...] = jnp.dot(h, w_ref[...], preferred_element_type=F32).astype(o_ref.dtype)

    @pl.when(j > 0)
    def _():
        o_ref[...] = jnp.dot(h_ref[...], w_ref[...], preferred_element_type=F32).astype(o_ref.dtype)


def _norm_matmul(x, g, w, layer, tm, tn):
    t, d = x.shape
    n = w.shape[2]
    return pl.pallas_call(
        _norm_matmul_kernel,
        grid=(t // tm, n // tn),
        in_specs=[
            pl.BlockSpec((tm, d), lambda i, j: (i, 0)),
            pl.BlockSpec((1, d), lambda i, j: (0, 0)),
            pl.BlockSpec((None, d, tn), lambda i, j: (layer, 0, j)),
        ],
        out_specs=pl.BlockSpec((tm, tn), lambda i, j: (i, j)),
        out_shape=jax.ShapeDtypeStruct((t, n), BF16),
        scratch_shapes=[pltpu.VMEM((tm, d), BF16)],
        compiler_params=_params(("parallel", "arbitrary"), 48),
    )(x, g.reshape(1, d), w)


def _proj_residual_kernel(*refs):
    x_ref, o_ref = refs[0], refs[-1]
    pairs = refs[1:-1]
    acc = x_ref[...]
    for k in range(len(pairs) // 2):
        acc = acc + jnp.dot(pairs[2 * k][...], pairs[2 * k + 1][...], preferred_element_type=F32)
    o_ref[...] = acc


def _proj_residual(x, ys, w, layer, tm):
    t, d = x.shape
    kw = ys[0].shape[1]
    in_specs = [pl.BlockSpec((tm, d), lambda i: (i, 0))]
    args = [x]
    for k, y in enumerate(ys):
        in_specs.append(pl.BlockSpec((tm, kw), lambda i: (i, 0)))
        in_specs.append(pl.BlockSpec((None, kw, d), lambda i, k=k: (layer, k, 0)))
        args += [y, w]
    return pl.pallas_call(
        _proj_residual_kernel,
        grid=(t // tm,),
        in_specs=in_specs,
        out_specs=pl.BlockSpec((tm, d), lambda i: (i, 0)),
        out_shape=jax.ShapeDtypeStruct((t, d), F32),
        compiler_params=_params(("parallel",), 48),
    )(*args)


def _pool_kernel(um_ref, up_ref, un_ref, w_ref, sc_ref, o_ref, ext_ref, *, tt, lp, ls):
    r0 = pl.program_id(0) * tt
    start, length = _seq_info(r0, lp, ls)
    first = r0 == start
    last = r0 + tt == start + length
    ext_ref[0:HALO, :] = jnp.where(first, 0.0, up_ref[...].astype(F32))
    ext_ref[HALO:HALO + tt, :] = um_ref[...].astype(F32)
    ext_ref[HALO + tt:, :] = jnp.where(last, 0.0, un_ref[...].astype(F32))
    pos = (r0 - start) + lax.broadcasted_iota(jnp.int32, (tt, 1), 0)
    for g, w in enumerate(POOL_WINDOWS):
        cs = slice(g * POOL_GROUP, (g + 1) * POOL_GROUP)
        s = ext_ref[pl.ds(HALO - w // 2, tt), cs]
        for off in range(-w // 2 + 1, w // 2):
            s = s + ext_ref[pl.ds(HALO + off, tt), cs]
        cnt = (jnp.minimum(pos + w // 2, length) - jnp.maximum(pos - w // 2, 0)).astype(F32)
        p = s / cnt - ext_ref[pl.ds(HALO, tt), cs]
        y = jnp.dot(p.astype(BF16), w_ref[g], preferred_element_type=F32) * sc_ref[:, cs]
        o_ref[:, cs] = y.astype(BF16)


def _pool(u, pool_w, scale, tt, lp, ls):
    t = u.shape[0]
    nh = t // HALO
    hb = tt // HALO
    return pl.pallas_call(
        functools.partial(_pool_kernel, tt=tt, lp=lp, ls=ls),
        grid=(t // tt,),
        in_specs=[
            pl.BlockSpec((tt, POOL_WIDTH), lambda i: (i, 0)),
            pl.BlockSpec((HALO, POOL_WIDTH), lambda i: (jnp.maximum(i * hb - 1, 0), 0)),
            pl.BlockSpec((HALO, POOL_WIDTH), lambda i: (jnp.minimum((i + 1) * hb, nh - 1), 0)),
            pl.BlockSpec((len(POOL_WINDOWS), POOL_GROUP, POOL_GROUP), lambda i: (0, 0, 0)),
            pl.BlockSpec((1, POOL_WIDTH), lambda i: (0, 0)),
        ],
        out_specs=pl.BlockSpec((tt, POOL_WIDTH), lambda i: (i, 0)),
        out_shape=jax.ShapeDtypeStruct((t, POOL_WIDTH), BF16),
        scratch_shapes=[pltpu.VMEM((tt + 2 * HALO, POOL_WIDTH), F32)],
        compiler_params=_params(("parallel",), 32),
    )(u, u, u, pool_w, scale.reshape(1, POOL_WIDTH))


def _gate_kernel(m0, m1, m2, p0, p1, p2, n0, n1, n2, cw_ref, cb_ref,
                 vvb_ref, vvf_ref, x0_ref, nyq_ref, *, tt, lp, ls):
    r0 = pl.program_id(0) * tt
    start, length = _seq_info(r0, lp, ls)
    first = r0 == start
    last = r0 + tt == start + length
    row = lax.broadcasted_iota(jnp.int32, (tt, 1), 0)

    def conv(m_ref, p_ref, n_ref, q):
        cs = slice(q * HYENA_WIDTH, (q + 1) * HYENA_WIDTH)
        m = m_ref[...].astype(F32)
        before = jnp.where(first, 0.0, p_ref[...].astype(F32)[HALO - 1:HALO, :])
        after = jnp.where(last, 0.0, n_ref[...].astype(F32)[0:1, :])
        up = jnp.where(row == 0, before, pltpu.roll(m, 1, 0))
        dn = jnp.where(row == tt - 1, after, pltpu.roll(m, tt - 1, 0))
        return up * cw_ref[0:1, cs] + m * cw_ref[1:2, cs] + dn * cw_ref[2:3, cs] + cb_ref[:, cs]

    x0 = conv(m0, p0, n0, 0)
    x1 = conv(m1, p1, n1, 1)
    v = conv(m2, p2, n2, 2)
    vv = v * x1
    vvb_ref[...] = vv.astype(BF16)
    vvf_ref[...] = vv
    x0_ref[...] = x0
    sign = jnp.where((row & 1) == 0, 1.0, -1.0)
    part = (vv * sign).reshape(tt // 8, 8, HYENA_WIDTH).sum(axis=0)

    @pl.when(first)
    def _():
        nyq_ref[...] = part

    @pl.when(jnp.logical_not(first))
    def _():
        nyq_ref[...] += part


def _gate(u, conv_w, conv_b, tt, lp, ls, n_seq):
    t = u.shape[0]
    nh = t // HALO
    hb = tt // HALO
    c = HYENA_WIDTH

    def seq_of(i):
        r0 = i * tt
        return jnp.where(r0 < lp, 0, 1 + jnp.maximum(r0 - lp, 0) // ls)

    main = [pl.BlockSpec((tt, c), lambda i, q=q: (i, 1 + q)) for q in range(3)]
    prev = [pl.BlockSpec((HALO, c), lambda i, q=q: (jnp.maximum(i * hb - 1, 0), 1 + q)) for q in range(3)]
    nxt = [pl.BlockSpec((HALO, c), lambda i, q=q: (jnp.minimum((i + 1) * hb, nh - 1), 1 + q)) for q in range(3)]
    return pl.pallas_call(
        functools.partial(_gate_kernel, tt=tt, lp=lp, ls=ls),
        grid=(t // tt,),
        in_specs=main + prev + nxt + [
            pl.BlockSpec((3, 3 * c), lambda i: (0, 0)),
            pl.BlockSpec((1, 3 * c), lambda i: (0, 0)),
        ],
        out_specs=[
            pl.BlockSpec((tt, c), lambda i: (i, 0)),
            pl.BlockSpec((tt, c), lambda i: (i, 0)),
            pl.BlockSpec((tt, c), lambda i: (i, 0)),
            pl.BlockSpec((None, 8, c), lambda i: (seq_of(i), 0, 0)),
        ],
        out_shape=[
            jax.ShapeDtypeStruct((t, c), BF16),
            jax.ShapeDtypeStruct((t, c), F32),
            jax.ShapeDtypeStruct((t, c), F32),
            jax.ShapeDtypeStruct((n_seq, 8, c), F32),
        ],
        compiler_params=_params(("arbitrary",), 48),
    )(*([u] * 9), conv_w, conv_b.reshape(1, 3 * c))


def _filter_kernel(z_ref, w1, b1, w2, b2, w3, b3, w4ah, w4al, w4bh, w4bl, b4, fr_ref, dl_ref,
                   e_ref, o_ref, st_ref, *, tr, slab_major):
    i = pl.program_id(0)
    z = z_ref[...]
    fr = fr_ref[...]

    def dot(a, b):
        return jnp.dot(a, b, preferred_element_type=F32, precision=HIGHEST)

    def dot_bf16(a, b):
        return jnp.dot(a, b, preferred_element_type=F32)

    h = jnp.sin(fr * (dot(z, w1[...]) + b1[...]))
    h = jnp.sin(fr * (dot(h, w2[...]) + b2[...]))
    h = jnp.sin(fr * (dot(h, w3[...]) + b3[...]))
    hh_ = h.astype(BF16)
    hl_ = (h - hh_.astype(F32)).astype(BF16)

    def last(wh, wl):
        return (dot_bf16(hl_, wh[...]) + dot_bf16(hh_, wl[...])) + dot_bf16(hh_, wh[...]) + b4[...]

    hh = jnp.concatenate([last(w4ah, w4al), last(w4bh, w4bl)], axis=0)
    t = jnp.concatenate([z[:, 0:1], z[:, FILTER_LANES:FILTER_LANES + 1]], axis=0)
    decay = jnp.exp(-t * jnp.abs(dl_ref[...]))
    row = i * tr + lax.broadcasted_iota(jnp.int32, (tr, 1), 0)
    hf = hh[:, :HYENA_WIDTH] * decay
    hb = jnp.where(row == 0, 0.0, hh[:, HYENA_WIDTH:] * decay)
    e = hf + hb
    o = hf - hb
    if slab_major:
        for a in range(tr // DFT_INNER):
            e_ref[:, a, :] = e[a * DFT_INNER:(a + 1) * DFT_INNER].astype(e_ref.dtype)
            o_ref[:, a, :] = o[a * DFT_INNER:(a + 1) * DFT_INNER].astype(o_ref.dtype)
    else:
        e_ref[...] = e.astype(e_ref.dtype)
        o_ref[...] = o.astype(o_ref.dtype)
    sign = jnp.where((row & 1) == 0, 1.0, -1.0)
    norm_part = (jnp.abs(hf) + jnp.abs(hb)).reshape(tr // 8, 8, HYENA_WIDTH).sum(axis=0)
    nyq_part = (e * sign).reshape(tr // 8, 8, HYENA_WIDTH).sum(axis=0)

    @pl.when(i == 0)
    def _():
        st_ref[0] = norm_part
        st_ref[1] = nyq_part

    @pl.when(i > 0)
    def _():
        st_ref[0] += norm_part
        st_ref[1] += nyq_part


def _hyena_filter(length, fp, tr, slab_major):
    c = HYENA_WIDTH
    if slab_major:
        taps_spec = pl.BlockSpec((DFT_INNER, tr // DFT_INNER, c), lambda i: (0, i, 0))
        taps_shape = jax.ShapeDtypeStruct((DFT_INNER, length // DFT_INNER, c), F32)
    else:
        taps_spec = pl.BlockSpec((tr, c), lambda i: (i, 0))
        taps_shape = jax.ShapeDtypeStruct((length, c), BF16)
    t = jnp.linspace(0.0, 1.0, length, dtype=F32)[:, None]
    w = 2.0 * math.pi * jnp.arange(length, dtype=F32)[:, None] / length
    f = jnp.linspace(1e-4, HYENA_EMB_BANDS - 1, HYENA_EMB_BANDS, dtype=F32)[None, :]
    z = jnp.concatenate([t, jnp.cos(f * w), -jnp.sin(f * w)], axis=-1)
    z = jnp.pad(z, ((0, 0), (0, FILTER_LANES - z.shape[1])))
    z = z.reshape(length // tr, 2, tr // 2, FILTER_LANES).transpose(0, 2, 1, 3).reshape(length // 2, 2 * FILTER_LANES)
    min_decay = math.log(HYENA_TARGET) / HYENA_SLOW_DECAY
    max_decay = math.log(HYENA_TARGET) / HYENA_FAST_DECAY
    deltas = jnp.linspace(min_decay, max_decay, c, dtype=F32)[None, :]
    full = lambda shape: pl.BlockSpec(shape, lambda i: (0,) * len(shape))
    return pl.pallas_call(
        functools.partial(_filter_kernel, tr=tr, slab_major=slab_major),
        grid=(length // tr,),
        in_specs=([pl.BlockSpec((tr // 2, 2 * FILTER_LANES), lambda i: (i, 0))]
                  + [full(a.shape) for a in fp] + [full((1, c))]),
        out_specs=[taps_spec, taps_spec, full((2, 8, c))],
        out_shape=[taps_shape, taps_shape, jax.ShapeDtypeStruct((2, 8, c), F32)],
        compiler_params=_params(("arbitrary",), 56),
    )(z, *fp, deltas)


def _pad_filter_params(w1, b1, w2, b2, w3, b3, w4, b4, freq):
    hid = FILTER_LANES
    assert max(w1.shape + w2.shape) <= hid
    pad2 = lambda a, r, c: jnp.pad(a.astype(F32), ((0, r - a.shape[0]), (0, c - a.shape[1])))
    zero = jnp.zeros((hid, hid), F32)
    diag = lambda a: jnp.block([[pad2(a, hid, hid), zero], [zero, pad2(a, hid, hid)]])
    row = lambda a: jnp.tile(jnp.pad(a.astype(F32), (0, hid - a.shape[0])), 2).reshape(1, 2 * hid)
    w4p = pad2(w4, hid, w4.shape[1])
    zero4 = jnp.zeros_like(w4p)

    def hi_lo(w):
        hi = w.astype(BF16)
        return hi, (w - hi.astype(F32)).astype(BF16)

    return (diag(w1), row(b1), diag(w2), row(b2), diag(w3), row(b3),
            *hi_lo(jnp.concatenate([w4p, zero4], axis=0)), *hi_lo(jnp.concatenate([zero4, w4p], axis=0)),
            b4.astype(F32).reshape(1, -1), row(freq))


def _dft_mats(length):
    n = 2 * length
    k = jnp.arange(length, dtype=jnp.int32)[:, None]

    def table(cols):
        ang = ((k * cols[None, :]) % n).astype(F32) * (2.0 * math.pi / n)
        return jnp.cos(ang), jnp.sin(ang)

    ca, sa = table(jnp.arange(length // 128, dtype=jnp.int32) * 128)
    cb, sb = table(jnp.arange(128, dtype=jnp.int32))
    cos = ca[:, :, None] * cb[:, None, :] - sa[:, :, None] * sb[:, None, :]
    sin = sa[:, :, None] * cb[:, None, :] + ca[:, :, None] * sb[:, None, :]
    return cos.reshape(length, length).astype(BF16), sin.reshape(length, length).astype(BF16)


def _spectrum_scale(st_ref, k0, tm, length):
    norm = jnp.sum(st_ref[0], axis=0, keepdims=True)
    krow = k0 + lax.broadcasted_iota(jnp.int32, (tm, 1), 0)
    wk = jnp.where(krow == 0, 1.0, 2.0)
    return wk / (2.0 * length * norm)


def _dft_filter_kernel(c_ref, s_ref, e_ref, o_ref, st_ref, kr_ref, ki_ref, accr, acci, *, tm, length):
    kt = pl.program_id(1)

    @pl.when(kt == 0)
    def _():
        accr[...] = jnp.zeros_like(accr)
        acci[...] = jnp.zeros_like(acci)

    accr[...] += jnp.dot(c_ref[...], e_ref[...], preferred_element_type=F32)
    acci[...] += jnp.dot(s_ref[...], o_ref[...], preferred_element_type=F32)

    @pl.when(kt == pl.num_programs(1) - 1)
    def _():
        sc = _spectrum_scale(st_ref, pl.program_id(0) * tm, tm, length)
        kr_ref[...] = accr[...] * sc
        ki_ref[...] = -acci[...] * sc


def _dft_filter(cm, sm, e, o, st, tm, tk):
    length = cm.shape[0]
    c = HYENA_WIDTH
    return pl.pallas_call(
        functools.partial(_dft_filter_kernel, tm=tm, length=length),
        grid=(length // tm, length // tk),
        in_specs=[
            pl.BlockSpec((tm, tk), lambda m, k: (m, k)),
            pl.BlockSpec((tm, tk), lambda m, k: (m, k)),
            pl.BlockSpec((tk, c), lambda m, k: (k, 0)),
            pl.BlockSpec((tk, c), lambda m, k: (k, 0)),
            pl.BlockSpec((2, 8, c), lambda m, k: (0, 0, 0)),
        ],
        out_specs=[pl.BlockSpec((tm, c), lambda m, k: (m, 0))] * 2,
        out_shape=[jax.ShapeDtypeStruct((length, c), F32)] * 2,
        scratch_shapes=[pltpu.VMEM((tm, c), F32)] * 2,
        compiler_params=_params(("parallel", "arbitrary"), 48),
    )(cm, sm, e, o, st)


def _dft_fwd_kernel(c_ref, s_ref, v_ref, kr_ref, ki_ref, yr_ref, yi_ref, accr, acci):
    kt = pl.program_id(2)

    @pl.when(kt == 0)
    def _():
        accr[...] = jnp.zeros_like(accr)
        acci[...] = jnp.zeros_like(acci)

    v = v_ref[...]
    accr[...] += jnp.dot(c_ref[...], v, preferred_element_type=F32)
    acci[...] += jnp.dot(s_ref[...], v, preferred_element_type=F32)

    @pl.when(kt == pl.num_programs(2) - 1)
    def _():
        vr = accr[...]
        vi = -acci[...]
        kr = kr_ref[...]
        ki = ki_ref[...]
        yr_ref[...] = (vr * kr - vi * ki).astype(BF16)
        yi_ref[...] = (vr * ki + vi * kr).astype(BF16)


def _dft_fwd(cm, sm, vvb, kr, ki, row0, batch, tm, tk):
    length = cm.shape[0]
    c = HYENA_WIDTH
    nm, nk = length // tm, length // tk
    off_k = row0 // tk
    return pl.pallas_call(
        _dft_fwd_kernel,
        grid=(nm, batch, nk),
        in_specs=[
            pl.BlockSpec((tm, tk), lambda m, b, k: (m, k)),
            pl.BlockSpec((tm, tk), lambda m, b, k: (m, k)),
            pl.BlockSpec((tk, c), lambda m, b, k: (off_k + b * nk + k, 0)),
            pl.BlockSpec((tm, c), lambda m, b, k: (m, 0)),
            pl.BlockSpec((tm, c), lambda m, b, k: (m, 0)),
        ],
        out_specs=[pl.BlockSpec((tm, c), lambda m, b, k: (b * nm + m, 0))] * 2,
        out_shape=[jax.ShapeDtypeStruct((batch * length, c), BF16)] * 2,
        scratch_shapes=[pltpu.VMEM((tm, c), F32)] * 2,
        compiler_params=_params(("parallel", "parallel", "arbitrary"), 48),
    )(cm, sm, vvb, kr, ki)


def _dft_inv_kernel(c_ref, s_ref, yr_ref, yi_ref, st_ref, vn_ref, vv_ref, x0_ref, d_ref, o_ref, acc,
                    *, tm, length):
    kt = pl.program_id(2)

    @pl.when(kt == 0)
    def _():
        acc[...] = jnp.zeros_like(acc)

    acc[...] += (jnp.dot(c_ref[...], yr_ref[...], preferred_element_type=F32)
                 - jnp.dot(s_ref[...], yi_ref[...], preferred_element_type=F32))

    @pl.when(kt == pl.num_programs(2) - 1)
    def _():
        norm = jnp.sum(st_ref[0], axis=0, keepdims=True)
        knyq = jnp.sum(st_ref[1], axis=0, keepdims=True)
        vnyq = jnp.sum(vn_ref[...], axis=0, keepdims=True)
        ynyq = vnyq * knyq / (2.0 * length * norm)
        row = lax.broadcasted_iota(jnp.int32, (tm, 1), 0)
        sign = jnp.where((row & 1) == 0, 1.0, -1.0)
        y = acc[...] + sign * ynyq
        o_ref[...] = ((y + vv_ref[...] * d_ref[...]) * x0_ref[...]).astype(BF16)


def _dft_inv_aliased_kernel(base_ref, *refs, tm, length):
    del base_ref
    _dft_inv_kernel(*refs, tm=tm, length=length)


def _dft_inv(cm, sm, yr, yi, st, vnyq, vvf, x0, d, row0, batch, tm, tk, base=None):
    length = cm.shape[0]
    c = HYENA_WIDTH
    nm, nk = length // tm, length // tk
    off_m = row0 // tm
    seq0 = 0 if row0 == 0 else 1
    body = _dft_inv_kernel if base is None else _dft_inv_aliased_kernel
    lead_specs = [] if base is None else [pl.BlockSpec(memory_space=pl.ANY)]
    lead_args = [] if base is None else [base]
    return pl.pallas_call(
        functools.partial(body, tm=tm, length=length),
        grid=(nm, batch, nk),
        input_output_aliases={} if base is None else {0: 0},
        in_specs=lead_specs + [
            pl.BlockSpec((tm, tk), lambda m, b, k: (m, k)),
            pl.BlockSpec((tm, tk), lambda m, b, k: (m, k)),
            pl.BlockSpec((tk, c), lambda m, b, k: (b * nk + k, 0)),
            pl.BlockSpec((tk, c), lambda m, b, k: (b * nk + k, 0)),
            pl.BlockSpec((2, 8, c), lambda m, b, k: (0, 0, 0)),
            pl.BlockSpec((None, 8, c), lambda m, b, k: (seq0 + b, 0, 0)),
            pl.BlockSpec((tm, c), lambda m, b, k: (off_m + b * nm + m, 0)),
            pl.BlockSpec((tm, c), lambda m, b, k: (off_m + b * nm + m, 0)),
            pl.BlockSpec((1, c), lambda m, b, k: (0, 0)),
        ],
        out_specs=pl.BlockSpec((tm, c), lambda m, b, k: (off_m + b * nm + m, 0)),
        out_shape=jax.ShapeDtypeStruct((vvf.shape[0], c), BF16),
        scratch_shapes=[pltpu.VMEM((tm, c), F32)],
        compiler_params=_params(("parallel", "parallel", "arbitrary"), 48),
    )(*lead_args, cm, sm, yr, yi, st, vnyq, vvf, x0, d.reshape(1, c))


def _pack_pair(re, im):
    hi = lax.bitcast_convert_type(re.astype(BF16).astype(F32), jnp.uint32)
    lo = lax.bitcast_convert_type(im.astype(BF16).astype(F32), jnp.uint32)
    return (hi & jnp.uint32(0xFFFF0000)) | (lo >> 16)


def _unpack_stacked(p):
    re = lax.bitcast_convert_type(p & jnp.uint32(0xFFFF0000), F32)
    im = lax.bitcast_convert_type(p << 16, F32)
    return jnp.concatenate([re, im], axis=0).astype(BF16)


def _two_stage_tables(length):
    n = 2 * length
    n1 = n // DFT_INNER
    a = jnp.arange(n1, dtype=jnp.int32)
    ang1 = ((a[:, None] * a[None, :n1 // 2]) % n1).astype(F32) * (2.0 * math.pi / n1)
    c1, s1 = jnp.cos(ang1), jnp.sin(ang1)
    f1 = jnp.concatenate([c1, -s1], axis=0).astype(BF16)
    f3 = jnp.concatenate([c1.T, -s1.T], axis=1).astype(BF16)
    b = jnp.arange(DFT_INNER, dtype=jnp.int32)
    k = a[:, None, None] + n1 * b[None, :, None]
    ang2 = ((k * b[None, None, :]) % n).astype(F32) * (2.0 * math.pi / n)
    c2, s2 = jnp.cos(ang2), jnp.sin(ang2)
    m1 = jnp.concatenate([jnp.concatenate([c2, s2], axis=2), jnp.concatenate([-s2, c2], axis=2)], axis=1)
    return f1, f3, m1.astype(BF16), jnp.swapaxes(m1, 1, 2).astype(BF16)


def _s1_kernel(f_ref, x_ref, a_ref, *, tn, slab_major):
    n1 = f_ref.shape[0] // 2
    for j in range(tn):
        xj = (x_ref[j] if slab_major else x_ref[:, j, :]).astype(BF16)
        a = jnp.dot(f_ref[...], xj, preferred_element_type=F32)
        a_ref[:, j, :] = _pack_pair(a[:n1], a[n1:])


def _stage1(f1, x3, tn, slab_major):
    rows2, half = f1.shape
    c = x3.shape[2]
    x_spec = (pl.BlockSpec((tn, half, c), lambda i: (i, 0, 0)) if slab_major
              else pl.BlockSpec((half, tn, c), lambda i: (0, i, 0)))
    return pl.pallas_call(
        functools.partial(_s1_kernel, tn=tn, slab_major=slab_major),
        grid=(DFT_INNER // tn,),
        in_specs=[pl.BlockSpec((rows2, half), lambda i: (0, 0)), x_spec],
        out_specs=pl.BlockSpec((rows2 // 2, tn, c), lambda i: (0, i, 0)),
        out_shape=jax.ShapeDtypeStruct((rows2 // 2, DFT_INNER, c), jnp.uint32),
        compiler_params=_params(("parallel",), 40),
    )(f1, x3)


def _mid_filter_kernel(e_ref, o_ref, m1_ref, st_ref, k_ref, *, kb, length):
    scale = 1.0 / (2.0 * length * jnp.sum(st_ref[0], axis=0, keepdims=True))
    for q in range(kb):
        xe = jnp.dot(m1_ref[q], _unpack_stacked(e_ref[q]), preferred_element_type=F32)
        xo = jnp.dot(m1_ref[q], _unpack_stacked(o_ref[q]), preferred_element_type=F32)
        k_ref[q] = _pack_pair(xe[:DFT_INNER] * scale, xo[DFT_INNER:] * scale)


def _mid_filter(ae, ao, m1, st, kb, ct):
    n1 = m1.shape[0]
    c = HYENA_WIDTH
    blk = pl.BlockSpec((kb, DFT_INNER, ct), lambda i, j: (i, 0, j))
    return pl.pallas_call(
        functools.partial(_mid_filter_kernel, kb=kb, length=n1 * DFT_INNER // 2),
        grid=(n1 // kb, c // ct),
        in_specs=[blk, blk,
                  pl.BlockSpec((kb, 2 * DFT_INNER, 2 * DFT_INNER), lambda i, j: (i, 0, 0)),
                  pl.BlockSpec((2, 8, ct), lambda i, j: (0, 0, j))],
        out_specs=blk,
        out_shape=jax.ShapeDtypeStruct((n1, DFT_INNER, c), jnp.uint32),
        compiler_params=_params(("parallel", "parallel"), 40),
    )(ae, ao, m1, st)


def _mid_kernel(a_ref, m1_ref, m2_ref, k_ref, z_ref, *, kb):
    for q in range(kb):
        x = jnp.dot(m1_ref[q], _unpack_stacked(a_ref[q]), preferred_element_type=F32)
        xr, xi = x[:DFT_INNER], x[DFT_INNER:]
        kq = k_ref[q]
        kr = lax.bitcast_convert_type(kq & jnp.uint32(0xFFFF0000), F32)
        ki = lax.bitcast_convert_type(kq << 16, F32)
        y = jnp.concatenate([xr * kr - xi * ki, xr * ki + xi * kr], axis=0).astype(BF16)
        z = jnp.dot(m2_ref[q], y, preferred_element_type=F32)
        z_ref[:, q, :] = _pack_pair(z[:DFT_INNER], z[DFT_INNER:])


def _mid(a, m1, m2, kf, kb, ct):
    n1 = m1.shape[0]
    c = HYENA_WIDTH
    mat = pl.BlockSpec((kb, 2 * DFT_INNER, 2 * DFT_INNER), lambda i, j: (i, 0, 0))
    return pl.pallas_call(
        functools.partial(_mid_kernel, kb=kb),
        grid=(n1 // kb, c // ct),
        in_specs=[pl.BlockSpec((kb, DFT_INNER, ct), lambda i, j: (i, 0, j)), mat, mat,
                  pl.BlockSpec((kb, DFT_INNER, ct), lambda i, j: (i, 0, j))],
        out_specs=pl.BlockSpec((DFT_INNER, kb, ct), lambda i, j: (0, i, j)),
        out_shape=jax.ShapeDtypeStruct((DFT_INNER, n1, c), jnp.uint32),
        compiler_params=_params(("parallel", "parallel"), 40),
    )(a, m1, m2, kf)


def _s3_kernel(f_ref, z_ref, y_ref, *, tn):
    for j in range(tn):
        y_ref[:, j, :] = jnp.dot(f_ref[...], _unpack_stacked(z_ref[j]), preferred_element_type=F32)


def _stage3(f3, z, tn):
    half = f3.shape[0]
    n1, c = z.shape[1], z.shape[2]
    return pl.pallas_call(
        functools.partial(_s3_kernel, tn=tn),
        grid=(DFT_INNER // tn,),
        in_specs=[pl.BlockSpec((half, 2 * n1), lambda i: (0, 0)), pl.BlockSpec((tn, n1, c), lambda i: (i, 0, 0))],
        out_specs=pl.BlockSpec((half, tn, c), lambda i: (0, i, 0)),
        out_shape=jax.ShapeDtypeStruct((half, DFT_INNER, c), F32),
        compiler_params=_params(("parallel",), 40),
    )(f3, z)


def _hy_out_kernel(y_ref, vv_ref, x0_ref, d_ref, o_ref):
    o_ref[...] = ((y_ref[...] + vv_ref[...] * d_ref[...]) * x0_ref[...]).astype(BF16)


def _hy_out(y, vvf, x0, d, tt):
    c = HYENA_WIDTH
    row = pl.BlockSpec((tt, c), lambda i: (i, 0))
    return pl.pallas_call(
        _hy_out_kernel,
        grid=(y.shape[0] // tt,),
        in_specs=[row, row, row, pl.BlockSpec((1, c), lambda i: (0, 0))],
        out_specs=row,
        out_shape=jax.ShapeDtypeStruct((vvf.shape[0], c), BF16),
        compiler_params=_params(("parallel",), 32),
    )(y, vvf, x0, d.reshape(1, c))


def _t5_buckets(rel):
    nb = N_BUCKETS // 2
    max_exact = nb // 2
    ret = (rel > 0).astype(jnp.int32) * nb
    n = jnp.abs(rel)
    large = max_exact + (jnp.log(jnp.maximum(n, 1).astype(F32) / max_exact)
                         / math.log(MAX_DISTANCE / max_exact) * (nb - max_exact)).astype(jnp.int32)
    large = jnp.minimum(large, nb - 1)
    return ret + jnp.where(n < max_exact, n, large)


def _bias_kernel(bk_ref, rb_ref, o_ref):
    h = pl.program_id(0)
    bk = bk_ref[...]
    qi = lax.broadcasted_iota(jnp.int32, bk.shape, 0)
    pj = lax.broadcasted_iota(jnp.int32, bk.shape, 1) - BLOCK
    acc = jnp.zeros(bk.shape, F32)
    for b in range(N_BUCKETS):
        acc = jnp.where(bk == b, rb_ref[b, h], acc)
    o_ref[...] = jnp.where(jnp.abs(pj - qi) <= WINDOW, acc, MASKED)


def _attn_bias(rel_bias):
    qi = jnp.arange(BLOCK)[:, None]
    pj = jnp.arange(3 * BLOCK)[None, :] - BLOCK
    buckets = _t5_buckets(pj - qi).astype(jnp.int32)
    return pl.pallas_call(
        _bias_kernel,
        grid=(N_HEADS,),
        in_specs=[
            pl.BlockSpec((BLOCK, 3 * BLOCK), lambda h: (0, 0)),
            pl.BlockSpec(memory_space=pltpu.SMEM),
        ],
        out_specs=pl.BlockSpec((None, BLOCK, 3 * BLOCK), lambda h: (h, 0, 0)),
        out_shape=jax.ShapeDtypeStruct((N_HEADS, BLOCK, 3 * BLOCK), F32),
        compiler_params=_params(("arbitrary",), 16),
    )(buckets, rel_bias.astype(F32))


def _attn_kernel(q_ref, kp_ref, kc_ref, kn_ref, vp_ref, vc_ref, vn_ref, bias_ref, sink_ref, o_ref, *, lp, ls):
    r0 = pl.program_id(0) * BLOCK
    start, length = _seq_info(r0, lp, ls)
    first = r0 == start
    last = r0 + BLOCK == start + length
    rows = GQA_GROUP * BLOCK
    col = lax.broadcasted_iota(jnp.int32, (rows, 3 * BLOCK), 1)
    outside = (first & (col < BLOCK)) | (last & (col >= 2 * BLOCK))
    grp = lax.broadcasted_iota(jnp.int32, (rows, 1), 0) // BLOCK
    scale = 1.0 / math.sqrt(HEAD_DIM)
    for kh in range(N_KV_HEADS):
        hs = slice(kh * HEAD_DIM, (kh + 1) * HEAD_DIM)
        kcat = jnp.concatenate([kp_ref[:, hs], kc_ref[:, hs], kn_ref[:, hs]], axis=0)
        vcat = jnp.concatenate([vp_ref[:, hs], vc_ref[:, hs], vn_ref[:, hs]], axis=0)
        heads = [kh * GQA_GROUP + g for g in range(GQA_GROUP)]
        q4 = jnp.concatenate([q_ref[:, h * HEAD_DIM:(h + 1) * HEAD_DIM] for h in heads], axis=0)
        s = lax.dot_general(q4, kcat, (((1,), (1,)), ((), ())), preferred_element_type=F32) * scale
        s = jnp.where(outside, MASKED, s + bias_ref[kh])
        sink = jnp.zeros((rows, 1), F32)
        for g, h in enumerate(heads):
            sink = jnp.where(grp == g, sink_ref[h], sink)
        m = jnp.maximum(jnp.max(s, axis=-1, keepdims=True), sink)
        p = jnp.exp(s - m)
        denom = jnp.sum(p, axis=-1, keepdims=True) + jnp.exp(sink - m)
        o = jnp.dot(p.astype(BF16), vcat, preferred_element_type=F32) / denom
        for g, h in enumerate(heads):
            o_ref[:, h * HEAD_DIM:(h + 1) * HEAD_DIM] = o[g * BLOCK:(g + 1) * BLOCK].astype(BF16)


def _attention(qkv, bias, sink, lp, ls):
    t = qkv.shape[0]
    nb = t // BLOCK
    qw = N_HEADS * HEAD_DIM
    kvw = N_KV_HEADS * HEAD_DIM
    kcol = qw // kvw
    prev = lambda i: jnp.maximum(i - 1, 0)
    nxt = lambda i: jnp.minimum(i + 1, nb - 1)
    kv_specs = [pl.BlockSpec((BLOCK, kvw), lambda i, f=f, cb=cb: (f(i), cb))
                for cb in (kcol, kcol + 1) for f in (prev, lambda i: i, nxt)]
    return pl.pallas_call(
        functools.partial(_attn_kernel, lp=lp, ls=ls),
        grid=(nb,),
        in_specs=[pl.BlockSpec((BLOCK, qw), lambda i: (i, 0))] + kv_specs + [
            pl.BlockSpec((N_KV_HEADS, GQA_GROUP * BLOCK, 3 * BLOCK), lambda i: (0, 0, 0)),
            pl.BlockSpec(memory_space=pltpu.SMEM),
        ],
        out_specs=pl.BlockSpec((BLOCK, qw), lambda i: (i, 0)),
        out_shape=jax.ShapeDtypeStruct((t, qw), BF16),
        compiler_params=_params(("parallel",), 32),
    )(qkv, *([qkv] * 6), bias.reshape(N_KV_HEADS, GQA_GROUP * BLOCK, 3 * BLOCK), sink.astype(F32))


def _tiles(lp, ls, dff):
    tok = min(512, ls)
    return dict(
        tm=tok,
        fm=min(1024, ls),
        tf=512 if dff % 512 == 0 else dff,
        tn=min(1024, ls),
        tt=min(512, ls),
        dm=min(512, ls),
        dk=min(2048, ls),
        tn2=8,
        kb=8,
        ct=512,
    )


def _forward(x_prompt, x_sample, p):
    depth = p['norm_ffn1'].shape[0]
    lp = x_prompt.shape[0] * x_prompt.shape[1]
    assert x_prompt.shape[0] == 1
    bs, ls = x_sample.shape[0], x_sample.shape[1]
    n_seq = 1 + bs
    assert lp % ls == 0
    dff = p['ffn1_wo'].shape[1]
    tl = _tiles(lp, ls, dff)
    tm, tt = tl['tm'], tl['tt']

    total = lp + bs * ls
    bf = lambda a: a.astype(BF16)
    ffn1_wi, ffn1_wo = bf(p['ffn1_wi']), bf(p['ffn1_wo'])
    ffn2_wi, ffn2_wo = bf(p['ffn2_wi']), bf(p['ffn2_wo'])
    ab_w_in, ab_w_out, pool_w = bf(p['ab_w_in']), bf(p['ab_w_out']), bf(p['pool_w'])
    w_qkv, w_o = bf(p['attn_w_qkv']), bf(p['attn_w_o'])
    bias = _attn_bias(p['rel_bias'])
    segments = ((0, 1, lp), (lp, bs, ls))
    lengths = {lp, ls}
    tables = {length: _two_stage_tables(length) for length in lengths if length >= TWO_STAGE_MIN_LEN}
    mats = {length: _dft_mats(length) for length in lengths if length < TWO_STAGE_MIN_LEN}

    for layer in range(depth):
        ffn1 = functools.partial(_ffn, g=p['norm_ffn1'][layer], wi=ffn1_wi, wo=ffn1_wo, layer=layer,
                                 tm=tl['fm'], tf=tl['tf'])
        if layer == 0:
            x = ffn1(x_prompt.reshape(lp, D_MODEL), out_rows=total)
            x = ffn1(x_sample.reshape(bs * ls, D_MODEL), out_rows=total, out_row0=lp, base=x)
        else:
            x = ffn1(x)
        i = layer // 2
        if layer % 2 == 0:
            u = _norm_matmul(x, p['norm_mix'][layer], ab_w_in, i, tl['fm'], tl['tn'])
            ya = _pool(u, pool_w[i], p['pool_scale'][i], tt, lp, ls)
            vvb, vvf, x0, vnyq = _gate(u, p['hy_conv_w'][i].astype(F32), p['hy_conv_b'][i].astype(F32),
                                       tt, lp, ls, n_seq)
            fp = _pad_filter_params(p['hy_ff_w1'][i], p['hy_ff_b1'][i], p['hy_ff_w2'][i], p['hy_ff_b2'][i],
                                    p['hy_ff_w3'][i], p['hy_ff_b3'][i], p['hy_ff_w4'][i], p['hy_ff_b4'][i],
                                    p['hy_freq'][i])
            d = p['hy_d'][i].astype(F32)
            yb = None
            for row0, batch, length in segments:
                two_stage = length in tables
                e, o, st = _hyena_filter(length, fp, min(1024 if two_stage else 512, length), two_stage)
                if two_stage:
                    assert row0 == 0 and batch == 1 and yb is None
                    f1, f3, m1, m2 = tables[length]
                    tn, kb, ct = tl['tn2'], tl['kb'], tl['ct']
                    kf = _mid_filter(_stage1(f1, e, tn, True), _stage1(f1, o, tn, True), m1, st, kb, ct)
                    v3 = vvf.reshape(-1, DFT_INNER, HYENA_WIDTH)
                    z = _mid(_stage1(f1, v3, tn, False), m1, m2, kf, kb, ct)
                    y = _stage3(f3, z, tn).reshape(length, HYENA_WIDTH)
                    yb = _hy_out(y, vvf, x0, d, tt)
                else:
                    cm, sm = mats[length]
                    dm, dk = min(tl['dm'], length), min(tl['dk'], length)
                    kr, ki = _dft_filter(cm, sm, e, o, st, dm, dk)
                    yr, yi = _dft_fwd(cm, sm, vvb, kr, ki, row0, batch, dm, dk)
                    yb = _dft_inv(cm, sm, yr, yi, st, vnyq, vvf, x0, d, row0, batch, dm, dk, base=yb)
            x = _proj_residual(x, [ya, yb], ab_w_out, i, tm)
        else:
            qkv = _norm_matmul(x, p['norm_mix'][layer], w_qkv, i, tl['fm'], tl['tn'])
            o = _attention(qkv, bias, p['attn_sink'][i], lp, ls)
            x = _proj_residual(x, [o], w_o, i, tm)
        ffn2 = functools.partial(_ffn, x, p['norm_ffn2'][layer], ffn2_wi, ffn2_wo, layer, tl['fm'], tl['tf'])
        if layer < depth - 1:
            x = ffn2()
    y_prompt = ffn2(rows=lp, final_g=p['norm_final'])
    y_sample = ffn2(in_row0=lp, rows=bs * ls, final_g=p['norm_final'])
    return y_prompt.reshape(x_prompt.shape), y_sample.reshape(x_sample.shape)


def kernel(x_prompt, x_sample, norm_ffn1, ffn1_wi, ffn1_wo, norm_mix, ab_w_in, pool_w, pool_scale, hy_conv_w, hy_conv_b, hy_ff_w1, hy_ff_b1, hy_ff_w2, hy_ff_b2, hy_ff_w3, hy_ff_b3, hy_ff_w4, hy_ff_b4, hy_freq, hy_d, ab_w_out, attn_w_qkv, attn_w_o, attn_sink, rel_bias, norm_ffn2, ffn2_wi, ffn2_wo, norm_final):
    p = dict(norm_ffn1=norm_ffn1, ffn1_wi=ffn1_wi, ffn1_wo=ffn1_wo, norm_mix=norm_mix,
             ab_w_in=ab_w_in, pool_w=pool_w, pool_scale=pool_scale,
             hy_conv_w=hy_conv_w, hy_conv_b=hy_conv_b, hy_ff_w1=hy_ff_w1, hy_ff_b1=hy_ff_b1,
             hy_ff_w2=hy_ff_w2, hy_ff_b2=hy_ff_b2, hy_ff_w3=hy_ff_w3, hy_ff_b3=hy_ff_b3,
             hy_ff_w4=hy_ff_w4, hy_ff_b4=hy_ff_b4, hy_freq=hy_freq, hy_d=hy_d, ab_w_out=ab_w_out,
             attn_w_qkv=attn_w_qkv, attn_w_o=attn_w_o, attn_sink=attn_sink, rel_bias=rel_bias,
             norm_ffn2=norm_ffn2, ffn2_wi=ffn2_wi, ffn2_wo=ffn2_wo, norm_final=norm_final)
    return _forward(x_prompt, x_sample, p)
```

```python
import functools
import math

import jax
import jax.numpy as jnp
import numpy as np
from jax import lax
from jax.experimental import pallas as pl
from jax.experimental.pallas import tpu as pltpu

F32 = jnp.float32
BF16 = jnp.bfloat16
HIGHEST = lax.Precision.HIGHEST

D_MODEL = 2048
EPS = 1e-6
POOL_WIDTH = 1024
POOL_WINDOWS = (2, 4, 8, 16)
POOL_GROUP = 256
HYENA_WIDTH = 1024
HYENA_EMB_BANDS = 16
HYENA_FAST_DECAY = 0.3
HYENA_SLOW_DECAY = 1.5
HYENA_TARGET = 1e-2
N_HEADS = 16
N_KV_HEADS = 4
HEAD_DIM = 128
GQA_GROUP = 4
WINDOW = 128
BLOCK = 128
N_BUCKETS = 32
MAX_DISTANCE = 128
MASKED = -1e30
HALO = 16
MIB = 1024 * 1024
FILTER_LANES = 64
DFT_INNER = 128
TWO_STAGE_MIN_LEN = 4096
SLAB_BLOCK = 8


def _params(sem, vmem_mib):
    return pltpu.CompilerParams(dimension_semantics=sem, vmem_limit_bytes=vmem_mib * MIB)


def _seq_info(r0, lp, ls):
    in_prompt = r0 < lp
    idx = jnp.maximum(r0 - lp, 0) // ls
    start = jnp.where(in_prompt, 0, lp + idx * ls)
    length = jnp.where(in_prompt, lp, ls)
    return start, length


def _rmsnorm(x, g):
    ms = jnp.mean(x * x, axis=-1, keepdims=True)
    return x * lax.rsqrt(ms + EPS) * g


def _ffn_kernel(*refs, seam, final_norm):
    nx = 1 if seam is None else 2
    x_refs, (g_ref, wg_ref, wu_ref, wo_ref) = refs[:nx], refs[nx:nx + 4]
    o_ref, h_ref = refs[-2:]
    j = pl.program_id(1)

    def branch(h):
        gate = jnp.dot(h, wg_ref[...], preferred_element_type=F32)
        up = jnp.dot(h, wu_ref[...], preferred_element_type=F32)
        act = (gate * jax.nn.sigmoid(gate) * (0.5 * up)).astype(BF16)
        return jnp.dot(act, wo_ref[...], preferred_element_type=F32)

    @pl.when(j == 0)
    def _():
        x = _stacked_tile(x_refs, seam)
        h = _rmsnorm(x, g_ref[...]).astype(BF16)
        h_ref[...] = h
        o_ref[...] = x + branch(h)

    @pl.when(j > 0)
    def _():
        o_ref[...] += branch(h_ref[...])

    if final_norm:
        @pl.when(j == pl.num_programs(1) - 1)
        def _():
            o_ref[...] = _rmsnorm(o_ref[...], refs[nx + 4][...])


def _ffn(xs, g, wi, wo, layer, tm, tf, *, in_row0=0, rows=None, final_g=None):
    d = xs[0].shape[1]
    total = sum(a.shape[0] for a in xs)
    rows = total if rows is None else rows
    dff = wo.shape[1]
    nj = dff // tf
    if len(xs) == 1:
        i0 = in_row0 // tm
        x_specs, seam = [pl.BlockSpec((tm, d), lambda i, j: (i0 + i, 0))], None
    else:
        assert in_row0 == 0 and rows == total
        x_specs, seam = _stacked_specs(xs, tm)
    row_vec = pl.BlockSpec((1, d), lambda i, j: (0, 0))
    in_specs = x_specs + [
        row_vec,
        pl.BlockSpec((None, d, tf), lambda i, j: (layer, 0, j)),
        pl.BlockSpec((None, d, tf), lambda i, j: (layer, 0, j + nj)),
        pl.BlockSpec((None, tf, d), lambda i, j: (layer, j, 0)),
    ]
    args = list(xs) + [g.reshape(1, d), wi, wi, wo]
    if final_g is not None:
        in_specs.append(row_vec)
        args.append(final_g.reshape(1, d))
    return pl.pallas_call(
        functools.partial(_ffn_kernel, seam=seam, final_norm=final_g is not None),
        grid=(rows // tm, nj),
        in_specs=in_specs,
        out_specs=pl.BlockSpec((tm, d), lambda i, j: (i, 0)),
        out_shape=jax.ShapeDtypeStruct((rows, d), F32),
        scratch_shapes=[pltpu.VMEM((tm, d), BF16)],
        compiler_params=_params(("parallel", "arbitrary"), 56),
    )(*args)


def _norm_matmul_kernel(x_ref, g_ref, w_ref, o_ref, h_ref):
    j = pl.program_id(1)

    @pl.when(j == 0)
    def _():
        h = _rmsnorm(x_ref[...], g_ref[...]).astype(BF16)
        h_ref[...] = h
        o_ref[...] = jnp.dot(h, w_ref[...], preferred_element_type=F32).astype(o_ref.dtype)

    @pl.when(j > 0)
    def _():
        o_ref[...] = jnp.dot(h_ref[...], w_ref[...], preferred_element_type=F32).astype(o_ref.dtype)


def _norm_matmul(x, g, w, layer, tm, tn):
    t, d = x.shape
    n = w.shape[2]
    return pl.pallas_call(
        _norm_matmul_kernel,
        grid=(t // tm, n // tn),
        in_specs=[
            pl.BlockSpec((tm, d), lambda i, j: (i, 0)),
            pl.BlockSpec((1, d), lambda i, j: (0, 0)),
            pl.BlockSpec((None, d, tn), lambda i, j: (layer, 0, j)),
        ],
        out_specs=pl.BlockSpec((tm, tn), lambda i, j: (i, j)),
        out_shape=jax.ShapeDtypeStruct((t, n), BF16),
        scratch_shapes=[pltpu.VMEM((tm, d), BF16)],
        compiler_params=_params(("parallel", "arbitrary"), 48),
    )(x, g.reshape(1, d), w)


def _stacked_specs(pieces, tm):
    width = pieces[0].shape[1]
    if len(pieces) == 1:
        return [pl.BlockSpec((tm, width), lambda i, *_: (i, 0))], None
    n0 = pieces[0].shape[0] // tm
    return [pl.BlockSpec((tm, width), lambda i, *_: (jnp.minimum(i, n0 - 1), 0)),
            pl.BlockSpec((tm, width), lambda i, *_: (jnp.maximum(i - n0, 0), 0))], n0


def _stacked_tile(refs, seam):
    if seam is None:
        return refs[0][...]
    return jnp.where(pl.program_id(0) < seam, refs[0][...], refs[1][...])


def _proj_residual_kernel(*refs, seams):
    x_ref, o_ref = refs[0], refs[-1]
    rest = list(refs[1:-1])
    acc = x_ref[...]
    for seam in seams:
        n = 1 if seam is None else 2
        y = _stacked_tile(rest[:n], seam)
        acc = acc + jnp.dot(y, rest[n][...], preferred_element_type=F32)
        rest = rest[n + 1:]
    o_ref[...] = acc


def _proj_residual(x, ys, w, layer, tm):
    t, d = x.shape
    kw = ys[0][0].shape[1]
    in_specs = [pl.BlockSpec((tm, d), lambda i: (i, 0))]
    args = [x]
    seams = []
    for k, pieces in enumerate(ys):
        specs, seam = _stacked_specs(pieces, tm)
        in_specs += specs + [pl.BlockSpec((None, kw, d), lambda i, k=k: (layer, k, 0))]
        args += list(pieces) + [w]
        seams.append(seam)
    return pl.pallas_call(
        functools.partial(_proj_residual_kernel, seams=tuple(seams)),
        grid=(t // tm,),
        in_specs=in_specs,
        out_specs=pl.BlockSpec((tm, d), lambda i: (i, 0)),
        out_shape=jax.ShapeDtypeStruct((t, d), F32),
        compiler_params=_params(("parallel",), 48),
    )(*args)


def _pool_kernel(um_ref, up_ref, un_ref, w_ref, sc_ref, o_ref, ext_ref, *, tt, lp, ls):
    r0 = pl.program_id(0) * tt
    start, length = _seq_info(r0, lp, ls)
    first = r0 == start
    last = r0 + tt == start + length
    ext_ref[0:HALO, :] = jnp.where(first, 0.0, up_ref[...].astype(F32))
    ext_ref[HALO:HALO + tt, :] = um_ref[...].astype(F32)
    ext_ref[HALO + tt:, :] = jnp.where(last, 0.0, un_ref[...].astype(F32))
    pos = (r0 - start) + lax.broadcasted_iota(jnp.int32, (tt, 1), 0)
    for g, w in enumerate(POOL_WINDOWS):
        cs = slice(g * POOL_GROUP, (g + 1) * POOL_GROUP)
        s = ext_ref[pl.ds(HALO - w // 2, tt), cs]
        for off in range(-w // 2 + 1, w // 2):
            s = s + ext_ref[pl.ds(HALO + off, tt), cs]
        cnt = (jnp.minimum(pos + w // 2, length) - jnp.maximum(pos - w // 2, 0)).astype(F32)
        p = s / cnt - ext_ref[pl.ds(HALO, tt), cs]
        y = jnp.dot(p.astype(BF16), w_ref[g], preferred_element_type=F32) * sc_ref[:, cs]
        o_ref[:, cs] = y.astype(BF16)


def _pool(u, pool_w, scale, tt, lp, ls):
    t = u.shape[0]
    nh = t // HALO
    hb = tt // HALO
    return pl.pallas_call(
        functools.partial(_pool_kernel, tt=tt, lp=lp, ls=ls),
        grid=(t // tt,),
        in_specs=[
            pl.BlockSpec((tt, POOL_WIDTH), lambda i: (i, 0)),
            pl.BlockSpec((HALO, POOL_WIDTH), lambda i: (jnp.maximum(i * hb - 1, 0), 0)),
            pl.BlockSpec((HALO, POOL_WIDTH), lambda i: (jnp.minimum((i + 1) * hb, nh - 1), 0)),
            pl.BlockSpec((len(POOL_WINDOWS), POOL_GROUP, POOL_GROUP), lambda i: (0, 0, 0)),
            pl.BlockSpec((1, POOL_WIDTH), lambda i: (0, 0)),
        ],
        out_specs=pl.BlockSpec((tt, POOL_WIDTH), lambda i: (i, 0)),
        out_shape=jax.ShapeDtypeStruct((t, POOL_WIDTH), BF16),
        scratch_shapes=[pltpu.VMEM((tt + 2 * HALO, POOL_WIDTH), F32)],
        compiler_params=_params(("parallel",), 32),
    )(u, u, u, pool_w, scale.reshape(1, POOL_WIDTH))


def _gate_kernel(m0, m1, m2, p0, p1, p2, n0, n1, n2, cw_ref, cb_ref,
                 vvb_ref, vvf_ref, x0_ref, nyq_ref, *slab_refs, tt, lp, ls, slab_rows):
    r0 = pl.program_id(0) * tt
    start, length = _seq_info(r0, lp, ls)
    first = r0 == start
    last = r0 + tt == start + length
    row = lax.broadcasted_iota(jnp.int32, (tt, 1), 0)

    def conv(m_ref, p_ref, n_ref, q):
        cs = slice(q * HYENA_WIDTH, (q + 1) * HYENA_WIDTH)
        m = m_ref[...].astype(F32)
        before = jnp.where(first, 0.0, p_ref[...].astype(F32)[HALO - 1:HALO, :])
        after = jnp.where(last, 0.0, n_ref[...].astype(F32)[0:1, :])
        up = jnp.where(row == 0, before, pltpu.roll(m, 1, 0))
        dn = jnp.where(row == tt - 1, after, pltpu.roll(m, tt - 1, 0))
        return up * cw_ref[0:1, cs] + m * cw_ref[1:2, cs] + dn * cw_ref[2:3, cs] + cb_ref[:, cs]

    x0 = conv(m0, p0, n0, 0)
    x1 = conv(m1, p1, n1, 1)
    v = conv(m2, p2, n2, 2)
    vv = v * x1
    vvb_ref[...] = vv.astype(BF16)
    vvf_ref[...] = vv
    x0_ref[...] = x0
    sign = jnp.where((row & 1) == 0, 1.0, -1.0)
    part = (vv * sign).reshape(tt // 8, 8, HYENA_WIDTH).sum(axis=0)

    @pl.when(first)
    def _():
        nyq_ref[...] = part

    @pl.when(jnp.logical_not(first))
    def _():
        nyq_ref[...] += part

    if slab_rows:
        per_tile = tt // DFT_INNER
        tile = pl.program_id(0)
        for phase in range(SLAB_BLOCK // per_tile):
            @pl.when((r0 < slab_rows) & (tile % (SLAB_BLOCK // per_tile) == phase))
            def _(phase=phase):
                for a in range(per_tile):
                    slab_refs[0][:, phase * per_tile + a, :] = vv[a * DFT_INNER:(a + 1) * DFT_INNER]


def _gate(u, conv_w, conv_b, tt, lp, ls, n_seq, slab_rows):
    t = u.shape[0]
    nh = t // HALO
    hb = tt // HALO
    c = HYENA_WIDTH

    def seq_of(i):
        r0 = i * tt
        return jnp.where(r0 < lp, 0, 1 + jnp.maximum(r0 - lp, 0) // ls)

    main = [pl.BlockSpec((tt, c), lambda i, q=q: (i, 1 + q)) for q in range(3)]
    prev = [pl.BlockSpec((HALO, c), lambda i, q=q: (jnp.maximum(i * hb - 1, 0), 1 + q)) for q in range(3)]
    nxt = [pl.BlockSpec((HALO, c), lambda i, q=q: (jnp.minimum((i + 1) * hb, nh - 1), 1 + q)) for q in range(3)]
    out_specs = [
        pl.BlockSpec((tt, c), lambda i: (i, 0)),
        pl.BlockSpec((tt, c), lambda i: (i, 0)),
        pl.BlockSpec((tt, c), lambda i: (i, 0)),
        pl.BlockSpec((None, 8, c), lambda i: (seq_of(i), 0, 0)),
    ]
    out_shape = [
        jax.ShapeDtypeStruct((t, c), BF16),
        jax.ShapeDtypeStruct((t, c), F32),
        jax.ShapeDtypeStruct((t, c), F32),
        jax.ShapeDtypeStruct((n_seq, 8, c), F32),
    ]
    if slab_rows:
        tiles_per_block = SLAB_BLOCK * DFT_INNER // tt
        last_block = slab_rows // (SLAB_BLOCK * DFT_INNER) - 1
        out_specs.append(pl.BlockSpec((DFT_INNER, SLAB_BLOCK, c),
                                      lambda i: (0, jnp.minimum(i // tiles_per_block, last_block), 0)))
        out_shape.append(jax.ShapeDtypeStruct((DFT_INNER, slab_rows // DFT_INNER, c), F32))
    return pl.pallas_call(
        functools.partial(_gate_kernel, tt=tt, lp=lp, ls=ls, slab_rows=slab_rows),
        grid=(t // tt,),
        in_specs=main + prev + nxt + [
            pl.BlockSpec((3, 3 * c), lambda i: (0, 0)),
            pl.BlockSpec((1, 3 * c), lambda i: (0, 0)),
        ],
        out_specs=out_specs,
        out_shape=out_shape,
        compiler_params=_params(("arbitrary",), 56),
    )(*([u] * 9), conv_w, conv_b.reshape(1, 3 * c))


def _filter_kernel(z_ref, w1, b1, w2, b2, w3, b3, w4ah, w4al, w4bh, w4bl, b4, fr_ref, dl_ref,
                   e_ref, o_ref, st_ref, *, tr, slab_major):
    i = pl.program_id(0)
    z = z_ref[...]
    fr = fr_ref[...]

    def dot(a, b):
        return jnp.dot(a, b, preferred_element_type=F32, precision=HIGHEST)

    def dot_bf16(a, b):
        return jnp.dot(a, b, preferred_element_type=F32)

    h = jnp.sin(fr * (dot(z, w1[...]) + b1[...]))
    h = jnp.sin(fr * (dot(h, w2[...]) + b2[...]))
    h = jnp.sin(fr * (dot(h, w3[...]) + b3[...]))
    hh_ = h.astype(BF16)
    hl_ = (h - hh_.astype(F32)).astype(BF16)

    def last(wh, wl):
        return (dot_bf16(hl_, wh[...]) + dot_bf16(hh_, wl[...])) + dot_bf16(hh_, wh[...]) + b4[...]

    hh = jnp.concatenate([last(w4ah, w4al), last(w4bh, w4bl)], axis=0)
    t = jnp.concatenate([z[:, 0:1], z[:, FILTER_LANES:FILTER_LANES + 1]], axis=0)
    decay = jnp.exp(-t * jnp.abs(dl_ref[...]))
    row = i * tr + lax.broadcasted_iota(jnp.int32, (tr, 1), 0)
    hf = hh[:, :HYENA_WIDTH] * decay
    hb = jnp.where(row == 0, 0.0, hh[:, HYENA_WIDTH:] * decay)
    e = hf + hb
    o = hf - hb
    if slab_major:
        for a in range(tr // DFT_INNER):
            e_ref[:, a, :] = e[a * DFT_INNER:(a + 1) * DFT_INNER].astype(e_ref.dtype)
            o_ref[:, a, :] = o[a * DFT_INNER:(a + 1) * DFT_INNER].astype(o_ref.dtype)
    else:
        e_ref[...] = e.astype(e_ref.dtype)
        o_ref[...] = o.astype(o_ref.dtype)
    sign = jnp.where((row & 1) == 0, 1.0, -1.0)
    norm_part = (jnp.abs(hf) + jnp.abs(hb)).reshape(tr // 8, 8, HYENA_WIDTH).sum(axis=0)
    nyq_part = (e * sign).reshape(tr // 8, 8, HYENA_WIDTH).sum(axis=0)

    @pl.when(i == 0)
    def _():
        st_ref[0] = norm_part
        st_ref[1] = nyq_part

    @pl.when(i > 0)
    def _():
        st_ref[0] += norm_part
        st_ref[1] += nyq_part


def _hyena_filter(length, fp, tr, slab_major):
    c = HYENA_WIDTH
    if slab_major:
        taps_spec = pl.BlockSpec((DFT_INNER, tr // DFT_INNER, c), lambda i: (0, i, 0))
        taps_shape = jax.ShapeDtypeStruct((DFT_INNER, length // DFT_INNER, c), F32)
    else:
        taps_spec = pl.BlockSpec((tr, c), lambda i: (i, 0))
        taps_shape = jax.ShapeDtypeStruct((length, c), BF16)
    t = jnp.linspace(0.0, 1.0, length, dtype=F32)[:, None]
    w = 2.0 * math.pi * jnp.arange(length, dtype=F32)[:, None] / length
    f = jnp.linspace(1e-4, HYENA_EMB_BANDS - 1, HYENA_EMB_BANDS, dtype=F32)[None, :]
    z = jnp.concatenate([t, jnp.cos(f * w), -jnp.sin(f * w)], axis=-1)
    z = jnp.pad(z, ((0, 0), (0, FILTER_LANES - z.shape[1])))
    z = z.reshape(length // tr, 2, tr // 2, FILTER_LANES).transpose(0, 2, 1, 3).reshape(length // 2, 2 * FILTER_LANES)
    min_decay = math.log(HYENA_TARGET) / HYENA_SLOW_DECAY
    max_decay = math.log(HYENA_TARGET) / HYENA_FAST_DECAY
    deltas = jnp.linspace(min_decay, max_decay, c, dtype=F32)[None, :]
    full = lambda shape: pl.BlockSpec(shape, lambda i: (0,) * len(shape))
    return pl.pallas_call(
        functools.partial(_filter_kernel, tr=tr, slab_major=slab_major),
        grid=(length // tr,),
        in_specs=([pl.BlockSpec((tr // 2, 2 * FILTER_LANES), lambda i: (i, 0))]
                  + [full(a.shape) for a in fp] + [full((1, c))]),
        out_specs=[taps_spec, taps_spec, full((2, 8, c))],
        out_shape=[taps_shape, taps_shape, jax.ShapeDtypeStruct((2, 8, c), F32)],
        compiler_params=_params(("arbitrary",), 56),
    )(z, *fp, deltas)


def _pad_filter_params(w1, b1, w2, b2, w3, b3, w4, b4, freq):
    hid = FILTER_LANES
    assert max(w1.shape + w2.shape) <= hid
    pad2 = lambda a, r, c: jnp.pad(a.astype(F32), ((0, r - a.shape[0]), (0, c - a.shape[1])))
    zero = jnp.zeros((hid, hid), F32)
    diag = lambda a: jnp.block([[pad2(a, hid, hid), zero], [zero, pad2(a, hid, hid)]])
    row = lambda a: jnp.tile(jnp.pad(a.astype(F32), (0, hid - a.shape[0])), 2).reshape(1, 2 * hid)
    w4p = pad2(w4, hid, w4.shape[1])
    zero4 = jnp.zeros_like(w4p)

    def hi_lo(w):
        hi = w.astype(BF16)
        return hi, (w - hi.astype(F32)).astype(BF16)

    return (diag(w1), row(b1), diag(w2), row(b2), diag(w3), row(b3),
            *hi_lo(jnp.concatenate([w4p, zero4], axis=0)), *hi_lo(jnp.concatenate([zero4, w4p], axis=0)),
            b4.astype(F32).reshape(1, -1), row(freq))


def _dft_mats(length):
    n = 2 * length
    k = jnp.arange(length, dtype=jnp.int32)[:, None]

    def table(cols):
        ang = ((k * cols[None, :]) % n).astype(F32) * (2.0 * math.pi / n)
        return jnp.cos(ang), jnp.sin(ang)

    ca, sa = table(jnp.arange(length // 128, dtype=jnp.int32) * 128)
    cb, sb = table(jnp.arange(128, dtype=jnp.int32))
    cos = ca[:, :, None] * cb[:, None, :] - sa[:, :, None] * sb[:, None, :]
    sin = sa[:, :, None] * cb[:, None, :] + ca[:, :, None] * sb[:, None, :]
    return cos.reshape(length, length).astype(BF16), sin.reshape(length, length).astype(BF16)


def _spectrum_scale(st_ref, k0, tm, length):
    norm = jnp.sum(st_ref[0], axis=0, keepdims=True)
    krow = k0 + lax.broadcasted_iota(jnp.int32, (tm, 1), 0)
    wk = jnp.where(krow == 0, 1.0, 2.0)
    return wk / (2.0 * length * norm)


def _dft_filter_kernel(c_ref, s_ref, e_ref, o_ref, st_ref, kr_ref, ki_ref, accr, acci, *, tm, length):
    kt = pl.program_id(1)

    @pl.when(kt == 0)
    def _():
        accr[...] = jnp.zeros_like(accr)
        acci[...] = jnp.zeros_like(acci)

    accr[...] += jnp.dot(c_ref[...], e_ref[...], preferred_element_type=F32)
    acci[...] += jnp.dot(s_ref[...], o_ref[...], preferred_element_type=F32)

    @pl.when(kt == pl.num_programs(1) - 1)
    def _():
        sc = _spectrum_scale(st_ref, pl.program_id(0) * tm, tm, length)
        kr_ref[...] = accr[...] * sc
        ki_ref[...] = -acci[...] * sc


def _dft_filter(cm, sm, e, o, st, tm, tk):
    length = cm.shape[0]
    c = HYENA_WIDTH
    return pl.pallas_call(
        functools.partial(_dft_filter_kernel, tm=tm, length=length),
        grid=(length // tm, length // tk),
        in_specs=[
            pl.BlockSpec((tm, tk), lambda m, k: (m, k)),
            pl.BlockSpec((tm, tk), lambda m, k: (m, k)),
            pl.BlockSpec((tk, c), lambda m, k: (k, 0)),
            pl.BlockSpec((tk, c), lambda m, k: (k, 0)),
            pl.BlockSpec((2, 8, c), lambda m, k: (0, 0, 0)),
        ],
        out_specs=[pl.BlockSpec((tm, c), lambda m, k: (m, 0))] * 2,
        out_shape=[jax.ShapeDtypeStruct((length, c), F32)] * 2,
        scratch_shapes=[pltpu.VMEM((tm, c), F32)] * 2,
        compiler_params=_params(("parallel", "arbitrary"), 48),
    )(cm, sm, e, o, st)


def _dft_fwd_kernel(c_ref, s_ref, v_ref, kr_ref, ki_ref, yr_ref, yi_ref, accr, acci):
    kt = pl.program_id(2)

    @pl.when(kt == 0)
    def _():
        accr[...] = jnp.zeros_like(accr)
        acci[...] = jnp.zeros_like(acci)

    v = v_ref[...]
    accr[...] += jnp.dot(c_ref[...], v, preferred_element_type=F32)
    acci[...] += jnp.dot(s_ref[...], v, preferred_element_type=F32)

    @pl.when(kt == pl.num_programs(2) - 1)
    def _():
        vr = accr[...]
        vi = -acci[...]
        kr = kr_ref[...]
        ki = ki_ref[...]
        yr_ref[...] = (vr * kr - vi * ki).astype(BF16)
        yi_ref[...] = (vr * ki + vi * kr).astype(BF16)


def _dft_fwd(cm, sm, vvb, kr, ki, row0, batch, tm, tk):
    length = cm.shape[0]
    c = HYENA_WIDTH
    nm, nk = length // tm, length // tk
    off_k = row0 // tk
    return pl.pallas_call(
        _dft_fwd_kernel,
        grid=(nm, batch, nk),
        in_specs=[
            pl.BlockSpec((tm, tk), lambda m, b, k: (m, k)),
            pl.BlockSpec((tm, tk), lambda m, b, k: (m, k)),
            pl.BlockSpec((tk, c), lambda m, b, k: (off_k + b * nk + k, 0)),
            pl.BlockSpec((tm, c), lambda m, b, k: (m, 0)),
            pl.BlockSpec((tm, c), lambda m, b, k: (m, 0)),
        ],
        out_specs=[pl.BlockSpec((tm, c), lambda m, b, k: (b * nm + m, 0))] * 2,
        out_shape=[jax.ShapeDtypeStruct((batch * length, c), BF16)] * 2,
        scratch_shapes=[pltpu.VMEM((tm, c), F32)] * 2,
        compiler_params=_params(("parallel", "parallel", "arbitrary"), 48),
    )(cm, sm, vvb, kr, ki)


def _dft_inv_kernel(c_ref, s_ref, yr_ref, yi_ref, st_ref, vn_ref, vv_ref, x0_ref, d_ref, o_ref, acc,
                    *, tm, length):
    kt = pl.program_id(2)

    @pl.when(kt == 0)
    def _():
        acc[...] = jnp.zeros_like(acc)

    acc[...] += (jnp.dot(c_ref[...], yr_ref[...], preferred_element_type=F32)
                 - jnp.dot(s_ref[...], yi_ref[...], preferred_element_type=F32))

    @pl.when(kt == pl.num_programs(2) - 1)
    def _():
        norm = jnp.sum(st_ref[0], axis=0, keepdims=True)
        knyq = jnp.sum(st_ref[1], axis=0, keepdims=True)
        vnyq = jnp.sum(vn_ref[...], axis=0, keepdims=True)
        ynyq = vnyq * knyq / (2.0 * length * norm)
        row = lax.broadcasted_iota(jnp.int32, (tm, 1), 0)
        sign = jnp.where((row & 1) == 0, 1.0, -1.0)
        y = acc[...] + sign * ynyq
        o_ref[...] = ((y + vv_ref[...] * d_ref[...]) * x0_ref[...]).astype(BF16)


def _dft_inv(cm, sm, yr, yi, st, vnyq, vvf, x0, d, row0, batch, tm, tk):
    length = cm.shape[0]
    c = HYENA_WIDTH
    nm, nk = length // tm, length // tk
    off_m = row0 // tm
    seq0 = 0 if row0 == 0 else 1
    return pl.pallas_call(
        functools.partial(_dft_inv_kernel, tm=tm, length=length),
        grid=(nm, batch, nk),
        in_specs=[
            pl.BlockSpec((tm, tk), lambda m, b, k: (m, k)),
            pl.BlockSpec((tm, tk), lambda m, b, k: (m, k)),
            pl.BlockSpec((tk, c), lambda m, b, k: (b * nk + k, 0)),
            pl.BlockSpec((tk, c), lambda m, b, k: (b * nk + k, 0)),
            pl.BlockSpec((2, 8, c), lambda m, b, k: (0, 0, 0)),
            pl.BlockSpec((None, 8, c), lambda m, b, k: (seq0 + b, 0, 0)),
            pl.BlockSpec((tm, c), lambda m, b, k: (off_m + b * nm + m, 0)),
            pl.BlockSpec((tm, c), lambda m, b, k: (off_m + b * nm + m, 0)),
            pl.BlockSpec((1, c), lambda m, b, k: (0, 0)),
        ],
        out_specs=pl.BlockSpec((tm, c), lambda m, b, k: (b * nm + m, 0)),
        out_shape=jax.ShapeDtypeStruct((batch * length, c), BF16),
        scratch_shapes=[pltpu.VMEM((tm, c), F32)],
        compiler_params=_params(("parallel", "parallel", "arbitrary"), 48),
    )(cm, sm, yr, yi, st, vnyq, vvf, x0, d.reshape(1, c))


def _pack_pair(re, im):
    hi = lax.bitcast_convert_type(re.astype(BF16).astype(F32), jnp.uint32)
    lo = lax.bitcast_convert_type(im.astype(BF16).astype(F32), jnp.uint32)
    return (hi & jnp.uint32(0xFFFF0000)) | (lo >> 16)


def _unpack_stacked(p):
    re = lax.bitcast_convert_type(p & jnp.uint32(0xFFFF0000), F32)
    im = lax.bitcast_convert_type(p << 16, F32)
    return jnp.concatenate([re, im], axis=0).astype(BF16)


def _two_stage_tables(length):
    n = 2 * length
    n1 = n // DFT_INNER
    a = jnp.arange(n1, dtype=jnp.int32)
    ang1 = ((a[:, None] * a[None, :n1 // 2]) % n1).astype(F32) * (2.0 * math.pi / n1)
    c1, s1 = jnp.cos(ang1), jnp.sin(ang1)
    f1 = jnp.concatenate([c1, -s1], axis=0).astype(BF16)
    f3 = jnp.concatenate([c1.T, -s1.T], axis=1).astype(BF16)
    b = jnp.arange(DFT_INNER, dtype=jnp.int32)
    k = a[:, None, None] + n1 * b[None, :, None]
    ang2 = ((k * b[None, None, :]) % n).astype(F32) * (2.0 * math.pi / n)
    c2, s2 = jnp.cos(ang2), jnp.sin(ang2)
    m1 = jnp.concatenate([jnp.concatenate([c2, s2], axis=2), jnp.concatenate([-s2, c2], axis=2)], axis=1)
    return f1, f3, m1.astype(BF16), jnp.swapaxes(m1, 1, 2).astype(BF16)


def _s1_kernel(f_ref, x_ref, a_ref, *, tn):
    n1 = f_ref.shape[0] // 2
    for j in range(tn):
        a = jnp.dot(f_ref[...], x_ref[j].astype(BF16), preferred_element_type=F32)
        a_ref[:, j, :] = _pack_pair(a[:n1], a[n1:])


def _stage1(f1, x3, tn):
    rows2, half = f1.shape
    c = x3.shape[2]
    return pl.pallas_call(
        functools.partial(_s1_kernel, tn=tn),
        grid=(DFT_INNER // tn,),
        in_specs=[pl.BlockSpec((rows2, half), lambda i: (0, 0)), pl.BlockSpec((tn, half, c), lambda i: (i, 0, 0))],
        out_specs=pl.BlockSpec((rows2 // 2, tn, c), lambda i: (0, i, 0)),
        out_shape=jax.ShapeDtypeStruct((rows2 // 2, DFT_INNER, c), jnp.uint32),
        compiler_params=_params(("parallel",), 40),
    )(f1, x3)


def _mid_filter_kernel(e_ref, o_ref, m1_ref, st_ref, k_ref, *, kb, length):
    scale = 1.0 / (2.0 * length * jnp.sum(st_ref[0], axis=0, keepdims=True))
    for q in range(kb):
        xe = jnp.dot(m1_ref[q], _unpack_stacked(e_ref[q]), preferred_element_type=F32)
        xo = jnp.dot(m1_ref[q], _unpack_stacked(o_ref[q]), preferred_element_type=F32)
        k_ref[q] = _pack_pair(xe[:DFT_INNER] * scale, xo[DFT_INNER:] * scale)


def _mid_filter(ae, ao, m1, st, kb, ct):
    n1 = m1.shape[0]
    c = HYENA_WIDTH
    blk = pl.BlockSpec((kb, DFT_INNER, ct), lambda i, j: (i, 0, j))
    return pl.pallas_call(
        functools.partial(_mid_filter_kernel, kb=kb, length=n1 * DFT_INNER // 2),
        grid=(n1 // kb, c // ct),
        in_specs=[blk, blk,
                  pl.BlockSpec((kb, 2 * DFT_INNER, 2 * DFT_INNER), lambda i, j: (i, 0, 0)),
                  pl.BlockSpec((2, 8, ct), lambda i, j: (0, 0, j))],
        out_specs=blk,
        out_shape=jax.ShapeDtypeStruct((n1, DFT_INNER, c), jnp.uint32),
        compiler_params=_params(("parallel", "parallel"), 40),
    )(ae, ao, m1, st)


def _mid_kernel(a_ref, m1_ref, m2_ref, k_ref, z_ref, *, kb):
    for q in range(kb):
        x = jnp.dot(m1_ref[q], _unpack_stacked(a_ref[q]), preferred_element_type=F32)
        xr, xi = x[:DFT_INNER], x[DFT_INNER:]
        kq = k_ref[q]
        kr = lax.bitcast_convert_type(kq & jnp.uint32(0xFFFF0000), F32)
        ki = lax.bitcast_convert_type(kq << 16, F32)
        y = jnp.concatenate([xr * kr - xi * ki, xr * ki + xi * kr], axis=0).astype(BF16)
        z = jnp.dot(m2_ref[q], y, preferred_element_type=F32)
        z_ref[:, q, :] = _pack_pair(z[:DFT_INNER], z[DFT_INNER:])


def _mid(a, m1, m2, kf, kb, ct):
    n1 = m1.shape[0]
    c = HYENA_WIDTH
    mat = pl.BlockSpec((kb, 2 * DFT_INNER, 2 * DFT_INNER), lambda i, j: (i, 0, 0))
    return pl.pallas_call(
        functools.partial(_mid_kernel, kb=kb),
        grid=(n1 // kb, c // ct),
        in_specs=[pl.BlockSpec((kb, DFT_INNER, ct), lambda i, j: (i, 0, j)), mat, mat,
                  pl.BlockSpec((kb, DFT_INNER, ct), lambda i, j: (i, 0, j))],
        out_specs=pl.BlockSpec((DFT_INNER, kb, ct), lambda i, j: (0, i, j)),
        out_shape=jax.ShapeDtypeStruct((DFT_INNER, n1, c), jnp.uint32),
        compiler_params=_params(("parallel", "parallel"), 40),
    )(a, m1, m2, kf)


def _s3_kernel(f_ref, z_ref, y_ref, *, tn):
    for j in range(tn):
        y_ref[:, j, :] = jnp.dot(f_ref[...], _unpack_stacked(z_ref[j]), preferred_element_type=F32)


def _stage3(f3, z, tn):
    half = f3.shape[0]
    n1, c = z.shape[1], z.shape[2]
    return pl.pallas_call(
        functools.partial(_s3_kernel, tn=tn),
        grid=(DFT_INNER // tn,),
        in_specs=[pl.BlockSpec((half, 2 * n1), lambda i: (0, 0)), pl.BlockSpec((tn, n1, c), lambda i: (i, 0, 0))],
        out_specs=pl.BlockSpec((half, tn, c), lambda i: (0, i, 0)),
        out_shape=jax.ShapeDtypeStruct((half, DFT_INNER, c), F32),
        compiler_params=_params(("parallel",), 40),
    )(f3, z)


def _hy_out_kernel(y_ref, vv_ref, x0_ref, d_ref, o_ref):
    o_ref[...] = ((y_ref[...] + vv_ref[...] * d_ref[...]) * x0_ref[...]).astype(BF16)


def _hy_out(y, vvf, x0, d, tt):
    c = HYENA_WIDTH
    row = pl.BlockSpec((tt, c), lambda i: (i, 0))
    return pl.pallas_call(
        _hy_out_kernel,
        grid=(y.shape[0] // tt,),
        in_specs=[row, row, row, pl.BlockSpec((1, c), lambda i: (0, 0))],
        out_specs=row,
        out_shape=jax.ShapeDtypeStruct(y.shape, BF16),
        compiler_params=_params(("parallel",), 32),
    )(y, vvf, x0, d.reshape(1, c))


def _t5_buckets(rel):
    nb = N_BUCKETS // 2
    max_exact = nb // 2
    ret = (rel > 0).astype(jnp.int32) * nb
    n = jnp.abs(rel)
    large = max_exact + (jnp.log(jnp.maximum(n, 1).astype(F32) / max_exact)
                         / math.log(MAX_DISTANCE / max_exact) * (nb - max_exact)).astype(jnp.int32)
    large = jnp.minimum(large, nb - 1)
    return ret + jnp.where(n < max_exact, n, large)


def _bias_kernel(bk_ref, rb_ref, o_ref):
    h = pl.program_id(0)
    bk = bk_ref[...]
    qi = lax.broadcasted_iota(jnp.int32, bk.shape, 0)
    pj = lax.broadcasted_iota(jnp.int32, bk.shape, 1) - BLOCK
    acc = jnp.zeros(bk.shape, F32)
    for b in range(N_BUCKETS):
        acc = jnp.where(bk == b, rb_ref[b, h], acc)
    o_ref[...] = jnp.where(jnp.abs(pj - qi) <= WINDOW, acc, MASKED)


def _attn_bias(rel_bias):
    qi = jnp.arange(BLOCK)[:, None]
    pj = jnp.arange(3 * BLOCK)[None, :] - BLOCK
    buckets = _t5_buckets(pj - qi).astype(jnp.int32)
    return pl.pallas_call(
        _bias_kernel,
        grid=(N_HEADS,),
        in_specs=[
            pl.BlockSpec((BLOCK, 3 * BLOCK), lambda h: (0, 0)),
            pl.BlockSpec(memory_space=pltpu.SMEM),
        ],
        out_specs=pl.BlockSpec((None, BLOCK, 3 * BLOCK), lambda h: (h, 0, 0)),
        out_shape=jax.ShapeDtypeStruct((N_HEADS, BLOCK, 3 * BLOCK), F32),
        compiler_params=_params(("arbitrary",), 16),
    )(buckets, rel_bias.astype(F32))


def _attn_kernel(q_ref, kp_ref, kc_ref, kn_ref, vp_ref, vc_ref, vn_ref, bias_ref, sink_ref, o_ref, *, lp, ls):
    r0 = pl.program_id(0) * BLOCK
    start, length = _seq_info(r0, lp, ls)
    first = r0 == start
    last = r0 + BLOCK == start + length
    rows = GQA_GROUP * BLOCK
    col = lax.broadcasted_iota(jnp.int32, (rows, 3 * BLOCK), 1)
    outside = (first & (col < BLOCK)) | (last & (col >= 2 * BLOCK))
    grp = lax.broadcasted_iota(jnp.int32, (rows, 1), 0) // BLOCK
    scale = 1.0 / math.sqrt(HEAD_DIM)
    for kh in range(N_KV_HEADS):
        hs = slice(kh * HEAD_DIM, (kh + 1) * HEAD_DIM)
        kcat = jnp.concatenate([kp_ref[:, hs], kc_ref[:, hs], kn_ref[:, hs]], axis=0)
        vcat = jnp.concatenate([vp_ref[:, hs], vc_ref[:, hs], vn_ref[:, hs]], axis=0)
        heads = [kh * GQA_GROUP + g for g in range(GQA_GROUP)]
        q4 = jnp.concatenate([q_ref[:, h * HEAD_DIM:(h + 1) * HEAD_DIM] for h in heads], axis=0)
        s = lax.dot_general(q4, kcat, (((1,), (1,)), ((), ())), preferred_element_type=F32) * scale
        s = jnp.where(outside, MASKED, s + bias_ref[kh])
        sink = jnp.zeros((rows, 1), F32)
        for g, h in enumerate(heads):
            sink = jnp.where(grp == g, sink_ref[h], sink)
        m = jnp.maximum(jnp.max(s, axis=-1, keepdims=True), sink)
        p = jnp.exp(s - m)
        denom = jnp.sum(p, axis=-1, keepdims=True) + jnp.exp(sink - m)
        o = jnp.dot(p.astype(BF16), vcat, preferred_element_type=F32) / denom
        for g, h in enumerate(heads):
            o_ref[:, h * HEAD_DIM:(h + 1) * HEAD_DIM] = o[g * BLOCK:(g + 1) * BLOCK].astype(BF16)


def _attention(qkv, bias, sink, lp, ls):
    t = qkv.shape[0]
    nb = t // BLOCK
    qw = N_HEADS * HEAD_DIM
    kvw = N_KV_HEADS * HEAD_DIM
    kcol = qw // kvw
    prev = lambda i: jnp.maximum(i - 1, 0)
    nxt = lambda i: jnp.minimum(i + 1, nb - 1)
    kv_specs = [pl.BlockSpec((BLOCK, kvw), lambda i, f=f, cb=cb: (f(i), cb))
                for cb in (kcol, kcol + 1) for f in (prev, lambda i: i, nxt)]
    return pl.pallas_call(
        functools.partial(_attn_kernel, lp=lp, ls=ls),
        grid=(nb,),
        in_specs=[pl.BlockSpec((BLOCK, qw), lambda i: (i, 0))] + kv_specs + [
            pl.BlockSpec((N_KV_HEADS, GQA_GROUP * BLOCK, 3 * BLOCK), lambda i: (0, 0, 0)),
            pl.BlockSpec(memory_space=pltpu.SMEM),
        ],
        out_specs=pl.BlockSpec((BLOCK, qw), lambda i: (i, 0)),
        out_shape=jax.ShapeDtypeStruct((t, qw), BF16),
        compiler_params=_params(("parallel",), 32),
    )(qkv, *([qkv] * 6), bias.reshape(N_KV_HEADS, GQA_GROUP * BLOCK, 3 * BLOCK), sink.astype(F32))


def _tiles(lp, ls, dff):
    tok = min(512, ls)
    return dict(
        tm=tok,
        fm=min(1024, ls),
        tf=512 if dff % 512 == 0 else dff,
        tn=min(1024, ls),
        tt=min(512, ls),
        dm=min(512, ls),
        dk=min(2048, ls),
        tn2=8,
        kb=8,
        ct=512,
    )


def _forward(x_prompt, x_sample, p):
    depth = p['norm_ffn1'].shape[0]
    lp = x_prompt.shape[0] * x_prompt.shape[1]
    assert x_prompt.shape[0] == 1
    bs, ls = x_sample.shape[0], x_sample.shape[1]
    n_seq = 1 + bs
    assert lp % ls == 0
    dff = p['ffn1_wo'].shape[1]
    tl = _tiles(lp, ls, dff)
    tm, tt = tl['tm'], tl['tt']

    bf = lambda a: a.astype(BF16)
    ffn1_wi, ffn1_wo = bf(p['ffn1_wi']), bf(p['ffn1_wo'])
    ffn2_wi, ffn2_wo = bf(p['ffn2_wi']), bf(p['ffn2_wo'])
    ab_w_in, ab_w_out, pool_w = bf(p['ab_w_in']), bf(p['ab_w_out']), bf(p['pool_w'])
    w_qkv, w_o = bf(p['attn_w_qkv']), bf(p['attn_w_o'])
    bias = _attn_bias(p['rel_bias'])
    segments = ((0, 1, lp), (lp, bs, ls))
    lengths = {lp, ls}
    tables = {length: _two_stage_tables(length) for length in lengths if length >= TWO_STAGE_MIN_LEN}
    mats = {length: _dft_mats(length) for length in lengths if length < TWO_STAGE_MIN_LEN}

    for layer in range(depth):
        xs = (x_prompt.reshape(lp, D_MODEL), x_sample.reshape(bs * ls, D_MODEL)) if layer == 0 else (x,)
        x = _ffn(xs, p['norm_ffn1'][layer], ffn1_wi, ffn1_wo, layer, tl['fm'] // len(xs), tl['tf'])
        i = layer // 2
        if layer % 2 == 0:
            u = _norm_matmul(x, p['norm_mix'][layer], ab_w_in, i, tl['fm'], tl['tn'])
            ya = _pool(u, pool_w[i], p['pool_scale'][i], tt, lp, ls)
            vvb, vvf, x0, vnyq, *vv_slabs = _gate(u, p['hy_conv_w'][i].astype(F32), p['hy_conv_b'][i].astype(F32),
                                                  tt, lp, ls, n_seq, lp if lp in tables else 0)
            fp = _pad_filter_params(p['hy_ff_w1'][i], p['hy_ff_b1'][i], p['hy_ff_w2'][i], p['hy_ff_b2'][i],
                                    p['hy_ff_w3'][i], p['hy_ff_b3'][i], p['hy_ff_w4'][i], p['hy_ff_b4'][i],
                                    p['hy_freq'][i])
            d = p['hy_d'][i].astype(F32)
            yb = []
            for row0, batch, length in segments:
                two_stage = length in tables
                e, o, st = _hyena_filter(length, fp, min(1024 if two_stage else 512, length), two_stage)
                if two_stage:
                    assert row0 == 0 and batch == 1
                    f1, f3, m1, m2 = tables[length]
                    tn, kb, ct = tl['tn2'], tl['kb'], tl['ct']
                    kf = _mid_filter(_stage1(f1, e, tn), _stage1(f1, o, tn), m1, st, kb, ct)
                    z = _mid(_stage1(f1, vv_slabs[0], tn), m1, m2, kf, kb, ct)
                    y = _stage3(f3, z, tn).reshape(length, HYENA_WIDTH)
                    yb.append(_hy_out(y, vvf, x0, d, tt))
                else:
                    cm, sm = mats[length]
                    dm, dk = min(tl['dm'], length), min(tl['dk'], length)
                    kr, ki = _dft_filter(cm, sm, e, o, st, dm, dk)
                    yr, yi = _dft_fwd(cm, sm, vvb, kr, ki, row0, batch, dm, dk)
                    yb.append(_dft_inv(cm, sm, yr, yi, st, vnyq, vvf, x0, d, row0, batch, dm, dk))
            x = _proj_residual(x, [(ya,), tuple(yb)], ab_w_out, i, tm)
        else:
            qkv = _norm_matmul(x, p['norm_mix'][layer], w_qkv, i, tl['fm'], tl['tn'])
            o = _attention(qkv, bias, p['attn_sink'][i], lp, ls)
            x = _proj_residual(x, [(o,)], w_o, i, tm)
        ffn2 = functools.partial(_ffn, (x,), p['norm_ffn2'][layer], ffn2_wi, ffn2_wo, layer, tl['fm'], tl['tf'])
        if layer < depth - 1:
            x = ffn2()
    y_prompt = ffn2(rows=lp, final_g=p['norm_final'])
    y_sample = ffn2(in_row0=lp, rows=bs * ls, final_g=p['norm_final'])
    return y_prompt.reshape(x_prompt.shape), y_sample.reshape(x_sample.shape)


def kernel(x_prompt, x_sample, norm_ffn1, ffn1_wi, ffn1_wo, norm_mix, ab_w_in, pool_w, pool_scale, hy_conv_w, hy_conv_b, hy_ff_w1, hy_ff_b1, hy_ff_w2, hy_ff_b2, hy_ff_w3, hy_ff_b3, hy_ff_w4, hy_ff_b4, hy_freq, hy_d, ab_w_out, attn_w_qkv, attn_w_o, attn_sink, rel_bias, norm_ffn2, ffn2_wi, ffn2_wo, norm_final):
    p = dict(norm_ffn1=norm_ffn1, ffn1_wi=ffn1_wi, ffn1_wo=ffn1_wo, norm_mix=norm_mix,
             ab_w_in=ab_w_in, pool_w=pool_w, pool_scale=pool_scale,
             hy_conv_w=hy_conv_w, hy_conv_b=hy_conv_b, hy_ff_w1=hy_ff_w1, hy_ff_b1=hy_ff_b1,
             hy_ff_w2=hy_ff_w2, hy_ff_b2=hy_ff_b2, hy_ff_w3=hy_ff_w3, hy_ff_b3=hy_ff_b3,
             hy_ff_w4=hy_ff_w4, hy_ff_b4=hy_ff_b4, hy_freq=hy_freq, hy_d=hy_d, ab_w_out=ab_w_out,
             attn_w_qkv=attn_w_qkv, attn_w_o=attn_w_o, attn_sink=attn_sink, rel_bias=rel_bias,
             norm_ffn2=norm_ffn2, ffn2_wi=ffn2_wi, ffn2_wo=ffn2_wo, norm_final=norm_final)
    return _forward(x_prompt, x_sample, p)
```

```python
import functools
import math

import jax
import jax.numpy as jnp
import numpy as np
from jax import lax
from jax.experimental import pallas as pl
from jax.experimental.pallas import tpu as pltpu

F32 = jnp.float32
BF16 = jnp.bfloat16
HIGHEST = lax.Precision.HIGHEST

D_MODEL = 2048
EPS = 1e-6
POOL_WIDTH = 1024
POOL_WINDOWS = (2, 4, 8, 16)
POOL_GROUP = 256
HYENA_WIDTH = 1024
HYENA_EMB_BANDS = 16
HYENA_FAST_DECAY = 0.3
HYENA_SLOW_DECAY = 1.5
HYENA_TARGET = 1e-2
N_HEADS = 16
N_KV_HEADS = 4
HEAD_DIM = 128
GQA_GROUP = 4
WINDOW = 128
BLOCK = 128
N_BUCKETS = 32
MAX_DISTANCE = 128
MASKED = -1e30
HALO = 16
MIB = 1024 * 1024
FILTER_LANES = 64
DFT_INNER = 128
TWO_STAGE_MIN_LEN = 4096
SLAB_BLOCK = 8


def _params(sem, vmem_mib):
    return pltpu.CompilerParams(dimension_semantics=sem, vmem_limit_bytes=vmem_mib * MIB)


def _seq_info(r0, lp, ls):
    in_prompt = r0 < lp
    idx = jnp.maximum(r0 - lp, 0) // ls
    start = jnp.where(in_prompt, 0, lp + idx * ls)
    length = jnp.where(in_prompt, lp, ls)
    return start, length


def _rmsnorm(x, g):
    ms = jnp.mean(x * x, axis=-1, keepdims=True)
    return x * lax.rsqrt(ms + EPS) * g


def _ffn_kernel(*refs, final_norm):
    x_ref, g_ref, wg_ref, wu_ref, wo_ref = refs[:5]
    o_ref, h_ref = refs[-2:]
    j = pl.program_id(1)

    def branch(h):
        gate = jnp.dot(h, wg_ref[...], preferred_element_type=F32)
        up = jnp.dot(h, wu_ref[...], preferred_element_type=F32)
        act = (gate * jax.nn.sigmoid(gate) * (0.5 * up)).astype(BF16)
        return jnp.dot(act, wo_ref[...], preferred_element_type=F32)

    @pl.when(j == 0)
    def _():
        h = _rmsnorm(x_ref[...], g_ref[...]).astype(BF16)
        h_ref[...] = h
        o_ref[...] = x_ref[...] + branch(h)

    @pl.when(j > 0)
    def _():
        o_ref[...] += branch(h_ref[...])

    if final_norm:
        @pl.when(j == pl.num_programs(1) - 1)
        def _():
            o_ref[...] = _rmsnorm(o_ref[...], refs[5][...])


def _ffn(x, g, wi, wo, layer, tm, tf, *, in_row0=0, rows=None, final_g=None):
    d = x.shape[1]
    rows = x.shape[0] if rows is None else rows
    dff = wo.shape[1]
    nj = dff // tf
    i0 = in_row0 // tm
    row_vec = pl.BlockSpec((1, d), lambda i, j: (0, 0))
    in_specs = [
        pl.BlockSpec((tm, d), lambda i, j: (i0 + i, 0)),
        row_vec,
        pl.BlockSpec((None, d, tf), lambda i, j: (layer, 0, j)),
        pl.BlockSpec((None, d, tf), lambda i, j: (layer, 0, j + nj)),
        pl.BlockSpec((None, tf, d), lambda i, j: (layer, j, 0)),
    ]
    args = [x, g.reshape(1, d), wi, wi, wo]
    if final_g is not None:
        in_specs.append(row_vec)
        args.append(final_g.reshape(1, d))
    return pl.pallas_call(
        functools.partial(_ffn_kernel, final_norm=final_g is not None),
        grid=(rows // tm, nj),
        in_specs=in_specs,
        out_specs=pl.BlockSpec((tm, d), lambda i, j: (i, 0)),
        out_shape=jax.ShapeDtypeStruct((rows, d), F32),
        scratch_shapes=[pltpu.VMEM((tm, d), BF16)],
        compiler_params=_params(("parallel", "arbitrary"), 56),
    )(*args)


def _norm_matmul_kernel(*refs, seam):
    nx = 1 if seam is None else 2
    x_refs, (g_ref, w_ref, o_ref, h_ref) = refs[:nx], refs[nx:]
    j = pl.program_id(1)

    @pl.when(j == 0)
    def _():
        h = _rmsnorm(_stacked_tile(x_refs, seam), g_ref[...]).astype(BF16)
        h_ref[...] = h
        o_ref[...] = jnp.dot(h, w_ref[...], preferred_element_type=F32).astype(o_ref.dtype)

    @pl.when(j > 0)
    def _():
        o_ref[...] = jnp.dot(h_ref[...], w_ref[...], preferred_element_type=F32).astype(o_ref.dtype)


def _norm_matmul(xs, g, w, layer, tm, tn):
    t, d = sum(a.shape[0] for a in xs), xs[0].shape[1]
    n = w.shape[2]
    x_specs, seam = _stacked_specs(xs, tm)
    return pl.pallas_call(
        functools.partial(_norm_matmul_kernel, seam=seam),
        grid=(t // tm, n // tn),
        in_specs=x_specs + [
            pl.BlockSpec((1, d), lambda i, j: (0, 0)),
            pl.BlockSpec((None, d, tn), lambda i, j: (layer, 0, j)),
        ],
        out_specs=pl.BlockSpec((tm, tn), lambda i, j: (i, j)),
        out_shape=jax.ShapeDtypeStruct((t, n), BF16),
        scratch_shapes=[pltpu.VMEM((tm, d), BF16)],
        compiler_params=_params(("parallel", "arbitrary"), 56),
    )(*xs, g.reshape(1, d), w)


def _stacked_specs(pieces, tm):
    width = pieces[0].shape[1]
    if len(pieces) == 1:
        return [pl.BlockSpec((tm, width), lambda i, *_: (i, 0))], None
    n0 = pieces[0].shape[0] // tm
    return [pl.BlockSpec((tm, width), lambda i, *_: (jnp.minimum(i, n0 - 1), 0)),
            pl.BlockSpec((tm, width), lambda i, *_: (jnp.maximum(i - n0, 0), 0))], n0


def _stacked_tile(refs, seam):
    if seam is None:
        return refs[0][...]
    return jnp.where(pl.program_id(0) < seam, refs[0][...], refs[1][...])


def _proj_residual_kernel(*refs, x_seam, seams):
    nx = 1 if x_seam is None else 2
    o_ref = refs[-1]
    rest = list(refs[nx:-1])
    acc = _stacked_tile(refs[:nx], x_seam)
    for seam in seams:
        n = 1 if seam is None else 2
        y = _stacked_tile(rest[:n], seam)
        acc = acc + jnp.dot(y, rest[n][...], preferred_element_type=F32)
        rest = rest[n + 1:]
    o_ref[...] = acc


def _proj_residual(xs, ys, w, layer, tm):
    t, d = sum(a.shape[0] for a in xs), xs[0].shape[1]
    kw = ys[0][0].shape[1]
    in_specs, x_seam = _stacked_specs(xs, tm)
    args = list(xs)
    seams = []
    for k, pieces in enumerate(ys):
        specs, seam = _stacked_specs(pieces, tm)
        in_specs += specs + [pl.BlockSpec((None, kw, d), lambda i, k=k: (layer, k, 0))]
        args += list(pieces) + [w]
        seams.append(seam)
    return pl.pallas_call(
        functools.partial(_proj_residual_kernel, x_seam=x_seam, seams=tuple(seams)),
        grid=(t // tm,),
        in_specs=in_specs,
        out_specs=pl.BlockSpec((tm, d), lambda i: (i, 0)),
        out_shape=jax.ShapeDtypeStruct((t, d), F32),
        compiler_params=_params(("parallel",), 48),
    )(*args)


def _pool_kernel(um_ref, up_ref, un_ref, w_ref, sc_ref, o_ref, ext_ref, *, tt, lp, ls):
    r0 = pl.program_id(0) * tt
    start, length = _seq_info(r0, lp, ls)
    first = r0 == start
    last = r0 + tt == start + length
    ext_ref[0:HALO, :] = jnp.where(first, 0.0, up_ref[...].astype(F32))
    ext_ref[HALO:HALO + tt, :] = um_ref[...].astype(F32)
    ext_ref[HALO + tt:, :] = jnp.where(last, 0.0, un_ref[...].astype(F32))
    pos = (r0 - start) + lax.broadcasted_iota(jnp.int32, (tt, 1), 0)
    for g, w in enumerate(POOL_WINDOWS):
        cs = slice(g * POOL_GROUP, (g + 1) * POOL_GROUP)
        s = ext_ref[pl.ds(HALO - w // 2, tt), cs]
        for off in range(-w // 2 + 1, w // 2):
            s = s + ext_ref[pl.ds(HALO + off, tt), cs]
        cnt = (jnp.minimum(pos + w // 2, length) - jnp.maximum(pos - w // 2, 0)).astype(F32)
        p = s / cnt - ext_ref[pl.ds(HALO, tt), cs]
        y = jnp.dot(p.astype(BF16), w_ref[g], preferred_element_type=F32) * sc_ref[:, cs]
        o_ref[:, cs] = y.astype(BF16)


def _pool(u, pool_w, scale, tt, lp, ls):
    t = u.shape[0]
    nh = t // HALO
    hb = tt // HALO
    return pl.pallas_call(
        functools.partial(_pool_kernel, tt=tt, lp=lp, ls=ls),
        grid=(t // tt,),
        in_specs=[
            pl.BlockSpec((tt, POOL_WIDTH), lambda i: (i, 0)),
            pl.BlockSpec((HALO, POOL_WIDTH), lambda i: (jnp.maximum(i * hb - 1, 0), 0)),
            pl.BlockSpec((HALO, POOL_WIDTH), lambda i: (jnp.minimum((i + 1) * hb, nh - 1), 0)),
            pl.BlockSpec((len(POOL_WINDOWS), POOL_GROUP, POOL_GROUP), lambda i: (0, 0, 0)),
            pl.BlockSpec((1, POOL_WIDTH), lambda i: (0, 0)),
        ],
        out_specs=pl.BlockSpec((tt, POOL_WIDTH), lambda i: (i, 0)),
        out_shape=jax.ShapeDtypeStruct((t, POOL_WIDTH), BF16),
        scratch_shapes=[pltpu.VMEM((tt + 2 * HALO, POOL_WIDTH), F32)],
        compiler_params=_params(("parallel",), 32),
    )(u, u, u, pool_w, scale.reshape(1, POOL_WIDTH))


def _gate_kernel(m0, m1, m2, p0, p1, p2, n0, n1, n2, cw_ref, cb_ref,
                 vvb_ref, vvf_ref, x0_ref, nyq_ref, *slab_refs, tt, lp, ls, slab_rows):
    r0 = pl.program_id(0) * tt
    start, length = _seq_info(r0, lp, ls)
    first = r0 == start
    last = r0 + tt == start + length
    row = lax.broadcasted_iota(jnp.int32, (tt, 1), 0)

    def conv(m_ref, p_ref, n_ref, q):
        cs = slice(q * HYENA_WIDTH, (q + 1) * HYENA_WIDTH)
        m = m_ref[...].astype(F32)
        before = jnp.where(first, 0.0, p_ref[...].astype(F32)[HALO - 1:HALO, :])
        after = jnp.where(last, 0.0, n_ref[...].astype(F32)[0:1, :])
        up = jnp.where(row == 0, before, pltpu.roll(m, 1, 0))
        dn = jnp.where(row == tt - 1, after, pltpu.roll(m, tt - 1, 0))
        return up * cw_ref[0:1, cs] + m * cw_ref[1:2, cs] + dn * cw_ref[2:3, cs] + cb_ref[:, cs]

    x0 = conv(m0, p0, n0, 0)
    x1 = conv(m1, p1, n1, 1)
    v = conv(m2, p2, n2, 2)
    vv = v * x1
    vvb_ref[...] = vv.astype(BF16)
    vvf_ref[...] = vv
    x0_ref[...] = x0
    sign = jnp.where((row & 1) == 0, 1.0, -1.0)
    part = (vv * sign).reshape(tt // 8, 8, HYENA_WIDTH).sum(axis=0)

    @pl.when(first)
    def _():
        nyq_ref[...] = part

    @pl.when(jnp.logical_not(first))
    def _():
        nyq_ref[...] += part

    if slab_rows:
        per_tile = tt // DFT_INNER
        tile = pl.program_id(0)
        for phase in range(SLAB_BLOCK // per_tile):
            @pl.when((r0 < slab_rows) & (tile % (SLAB_BLOCK // per_tile) == phase))
            def _(phase=phase):
                for a in range(per_tile):
                    slab_refs[0][:, phase * per_tile + a, :] = vv[a * DFT_INNER:(a + 1) * DFT_INNER]


def _gate(u, conv_w, conv_b, tt, lp, ls, n_seq, slab_rows):
    t = u.shape[0]
    nh = t // HALO
    hb = tt // HALO
    c = HYENA_WIDTH

    def seq_of(i):
        r0 = i * tt
        return jnp.where(r0 < lp, 0, 1 + jnp.maximum(r0 - lp, 0) // ls)

    main = [pl.BlockSpec((tt, c), lambda i, q=q: (i, 1 + q)) for q in range(3)]
    prev = [pl.BlockSpec((HALO, c), lambda i, q=q: (jnp.maximum(i * hb - 1, 0), 1 + q)) for q in range(3)]
    nxt = [pl.BlockSpec((HALO, c), lambda i, q=q: (jnp.minimum((i + 1) * hb, nh - 1), 1 + q)) for q in range(3)]
    out_specs = [
        pl.BlockSpec((tt, c), lambda i: (i, 0)),
        pl.BlockSpec((tt, c), lambda i: (i, 0)),
        pl.BlockSpec((tt, c), lambda i: (i, 0)),
        pl.BlockSpec((None, 8, c), lambda i: (seq_of(i), 0, 0)),
    ]
    out_shape = [
        jax.ShapeDtypeStruct((t, c), BF16),
        jax.ShapeDtypeStruct((t, c), F32),
        jax.ShapeDtypeStruct((t, c), F32),
        jax.ShapeDtypeStruct((n_seq, 8, c), F32),
    ]
    if slab_rows:
        tiles_per_block = SLAB_BLOCK * DFT_INNER // tt
        last_block = slab_rows // (SLAB_BLOCK * DFT_INNER) - 1
        out_specs.append(pl.BlockSpec((DFT_INNER, SLAB_BLOCK, c),
                                      lambda i: (0, jnp.minimum(i // tiles_per_block, last_block), 0)))
        out_shape.append(jax.ShapeDtypeStruct((DFT_INNER, slab_rows // DFT_INNER, c), F32))
    return pl.pallas_call(
        functools.partial(_gate_kernel, tt=tt, lp=lp, ls=ls, slab_rows=slab_rows),
        grid=(t // tt,),
        in_specs=main + prev + nxt + [
            pl.BlockSpec((3, 3 * c), lambda i: (0, 0)),
            pl.BlockSpec((1, 3 * c), lambda i: (0, 0)),
        ],
        out_specs=out_specs,
        out_shape=out_shape,
        compiler_params=_params(("arbitrary",), 56),
    )(*([u] * 9), conv_w, conv_b.reshape(1, 3 * c))


def _filter_kernel(z_ref, w1, b1, w2, b2, w3, b3, w4ah, w4al, w4bh, w4bl, b4, fr_ref, dl_ref,
                   e_ref, o_ref, st_ref, *, tr, slab_major):
    i = pl.program_id(0)
    z = z_ref[...]
    fr = fr_ref[...]

    def dot(a, b):
        return jnp.dot(a, b, preferred_element_type=F32, precision=HIGHEST)

    def dot_bf16(a, b):
        return jnp.dot(a, b, preferred_element_type=F32)

    h = jnp.sin(fr * (dot(z, w1[...]) + b1[...]))
    h = jnp.sin(fr * (dot(h, w2[...]) + b2[...]))
    h = jnp.sin(fr * (dot(h, w3[...]) + b3[...]))
    hh_ = h.astype(BF16)
    hl_ = (h - hh_.astype(F32)).astype(BF16)

    def last(wh, wl):
        return (dot_bf16(hl_, wh[...]) + dot_bf16(hh_, wl[...])) + dot_bf16(hh_, wh[...]) + b4[...]

    hh = jnp.concatenate([last(w4ah, w4al), last(w4bh, w4bl)], axis=0)
    t = jnp.concatenate([z[:, 0:1], z[:, FILTER_LANES:FILTER_LANES + 1]], axis=0)
    decay = jnp.exp(-t * jnp.abs(dl_ref[...]))
    row = i * tr + lax.broadcasted_iota(jnp.int32, (tr, 1), 0)
    hf = hh[:, :HYENA_WIDTH] * decay
    hb = jnp.where(row == 0, 0.0, hh[:, HYENA_WIDTH:] * decay)
    e = hf + hb
    o = hf - hb
    if slab_major:
        for a in range(tr // DFT_INNER):
            e_ref[:, a, :] = e[a * DFT_INNER:(a + 1) * DFT_INNER].astype(e_ref.dtype)
            o_ref[:, a, :] = o[a * DFT_INNER:(a + 1) * DFT_INNER].astype(o_ref.dtype)
    else:
        e_ref[...] = e.astype(e_ref.dtype)
        o_ref[...] = o.astype(o_ref.dtype)
    sign = jnp.where((row & 1) == 0, 1.0, -1.0)
    norm_part = (jnp.abs(hf) + jnp.abs(hb)).reshape(tr // 8, 8, HYENA_WIDTH).sum(axis=0)
    nyq_part = (e * sign).reshape(tr // 8, 8, HYENA_WIDTH).sum(axis=0)

    @pl.when(i == 0)
    def _():
        st_ref[0] = norm_part
        st_ref[1] = nyq_part

    @pl.when(i > 0)
    def _():
        st_ref[0] += norm_part
        st_ref[1] += nyq_part


def _hyena_filter(length, fp, tr, slab_major):
    c = HYENA_WIDTH
    if slab_major:
        taps_spec = pl.BlockSpec((DFT_INNER, tr // DFT_INNER, c), lambda i: (0, i, 0))
        taps_shape = jax.ShapeDtypeStruct((DFT_INNER, length // DFT_INNER, c), F32)
    else:
        taps_spec = pl.BlockSpec((tr, c), lambda i: (i, 0))
        taps_shape = jax.ShapeDtypeStruct((length, c), BF16)
    t = jnp.linspace(0.0, 1.0, length, dtype=F32)[:, None]
    w = 2.0 * math.pi * jnp.arange(length, dtype=F32)[:, None] / length
    f = jnp.linspace(1e-4, HYENA_EMB_BANDS - 1, HYENA_EMB_BANDS, dtype=F32)[None, :]
    z = jnp.concatenate([t, jnp.cos(f * w), -jnp.sin(f * w)], axis=-1)
    z = jnp.pad(z, ((0, 0), (0, FILTER_LANES - z.shape[1])))
    z = z.reshape(length // tr, 2, tr // 2, FILTER_LANES).transpose(0, 2, 1, 3).reshape(length // 2, 2 * FILTER_LANES)
    min_decay = math.log(HYENA_TARGET) / HYENA_SLOW_DECAY
    max_decay = math.log(HYENA_TARGET) / HYENA_FAST_DECAY
    deltas = jnp.linspace(min_decay, max_decay, c, dtype=F32)[None, :]
    full = lambda shape: pl.BlockSpec(shape, lambda i: (0,) * len(shape))
    return pl.pallas_call(
        functools.partial(_filter_kernel, tr=tr, slab_major=slab_major),
        grid=(length // tr,),
        in_specs=([pl.BlockSpec((tr // 2, 2 * FILTER_LANES), lambda i: (i, 0))]
                  + [full(a.shape) for a in fp] + [full((1, c))]),
        out_specs=[taps_spec, taps_spec, full((2, 8, c))],
        out_shape=[taps_shape, taps_shape, jax.ShapeDtypeStruct((2, 8, c), F32)],
        compiler_params=_params(("arbitrary",), 56),
    )(z, *fp, deltas)


def _pad_filter_params(w1, b1, w2, b2, w3, b3, w4, b4, freq):
    hid = FILTER_LANES
    assert max(w1.shape + w2.shape) <= hid
    pad2 = lambda a, r, c: jnp.pad(a.astype(F32), ((0, r - a.shape[0]), (0, c - a.shape[1])))
    zero = jnp.zeros((hid, hid), F32)
    diag = lambda a: jnp.block([[pad2(a, hid, hid), zero], [zero, pad2(a, hid, hid)]])
    row = lambda a: jnp.tile(jnp.pad(a.astype(F32), (0, hid - a.shape[0])), 2).reshape(1, 2 * hid)
    w4p = pad2(w4, hid, w4.shape[1])
    zero4 = jnp.zeros_like(w4p)

    def hi_lo(w):
        hi = w.astype(BF16)
        return hi, (w - hi.astype(F32)).astype(BF16)

    return (diag(w1), row(b1), diag(w2), row(b2), diag(w3), row(b3),
            *hi_lo(jnp.concatenate([w4p, zero4], axis=0)), *hi_lo(jnp.concatenate([zero4, w4p], axis=0)),
            b4.astype(F32).reshape(1, -1), row(freq))


def _dft_mats(length):
    n = 2 * length
    k = jnp.arange(length, dtype=jnp.int32)[:, None]

    def table(cols):
        ang = ((k * cols[None, :]) % n).astype(F32) * (2.0 * math.pi / n)
        return jnp.cos(ang), jnp.sin(ang)

    ca, sa = table(jnp.arange(length // 128, dtype=jnp.int32) * 128)
    cb, sb = table(jnp.arange(128, dtype=jnp.int32))
    cos = ca[:, :, None] * cb[:, None, :] - sa[:, :, None] * sb[:, None, :]
    sin = sa[:, :, None] * cb[:, None, :] + ca[:, :, None] * sb[:, None, :]
    return cos.reshape(length, length).astype(BF16), sin.reshape(length, length).astype(BF16)


def _spectrum_scale(st_ref, k0, tm, length):
    norm = jnp.sum(st_ref[0], axis=0, keepdims=True)
    krow = k0 + lax.broadcasted_iota(jnp.int32, (tm, 1), 0)
    wk = jnp.where(krow == 0, 1.0, 2.0)
    return wk / (2.0 * length * norm)


def _dft_filter_kernel(c_ref, s_ref, e_ref, o_ref, st_ref, kr_ref, ki_ref, accr, acci, *, tm, length):
    kt = pl.program_id(1)

    @pl.when(kt == 0)
    def _():
        accr[...] = jnp.zeros_like(accr)
        acci[...] = jnp.zeros_like(acci)

    accr[...] += jnp.dot(c_ref[...], e_ref[...], preferred_element_type=F32)
    acci[...] += jnp.dot(s_ref[...], o_ref[...], preferred_element_type=F32)

    @pl.when(kt == pl.num_programs(1) - 1)
    def _():
        sc = _spectrum_scale(st_ref, pl.program_id(0) * tm, tm, length)
        kr_ref[...] = accr[...] * sc
        ki_ref[...] = -acci[...] * sc


def _dft_filter(cm, sm, e, o, st, tm, tk):
    length = cm.shape[0]
    c = HYENA_WIDTH
    return pl.pallas_call(
        functools.partial(_dft_filter_kernel, tm=tm, length=length),
        grid=(length // tm, length // tk),
        in_specs=[
            pl.BlockSpec((tm, tk), lambda m, k: (m, k)),
            pl.BlockSpec((tm, tk), lambda m, k: (m, k)),
            pl.BlockSpec((tk, c), lambda m, k: (k, 0)),
            pl.BlockSpec((tk, c), lambda m, k: (k, 0)),
            pl.BlockSpec((2, 8, c), lambda m, k: (0, 0, 0)),
        ],
        out_specs=[pl.BlockSpec((tm, c), lambda m, k: (m, 0))] * 2,
        out_shape=[jax.ShapeDtypeStruct((length, c), F32)] * 2,
        scratch_shapes=[pltpu.VMEM((tm, c), F32)] * 2,
        compiler_params=_params(("parallel", "arbitrary"), 48),
    )(cm, sm, e, o, st)


def _dft_fwd_kernel(c_ref, s_ref, v_ref, kr_ref, ki_ref, yr_ref, yi_ref, accr, acci):
    kt = pl.program_id(2)

    @pl.when(kt == 0)
    def _():
        accr[...] = jnp.zeros_like(accr)
        acci[...] = jnp.zeros_like(acci)

    v = v_ref[...]
    accr[...] += jnp.dot(c_ref[...], v, preferred_element_type=F32)
    acci[...] += jnp.dot(s_ref[...], v, preferred_element_type=F32)

    @pl.when(kt == pl.num_programs(2) - 1)
    def _():
        vr = accr[...]
        vi = -acci[...]
        kr = kr_ref[...]
        ki = ki_ref[...]
        yr_ref[...] = (vr * kr - vi * ki).astype(BF16)
        yi_ref[...] = (vr * ki + vi * kr).astype(BF16)


def _dft_fwd(cm, sm, vvb, kr, ki, row0, batch, tm, tk):
    length = cm.shape[0]
    c = HYENA_WIDTH
    nm, nk = length // tm, length // tk
    off_k = row0 // tk
    return pl.pallas_call(
        _dft_fwd_kernel,
        grid=(nm, batch, nk),
        in_specs=[
            pl.BlockSpec((tm, tk), lambda m, b, k: (m, k)),
            pl.BlockSpec((tm, tk), lambda m, b, k: (m, k)),
            pl.BlockSpec((tk, c), lambda m, b, k: (off_k + b * nk + k, 0)),
            pl.BlockSpec((tm, c), lambda m, b, k: (m, 0)),
            pl.BlockSpec((tm, c), lambda m, b, k: (m, 0)),
        ],
        out_specs=[pl.BlockSpec((tm, c), lambda m, b, k: (b * nm + m, 0))] * 2,
        out_shape=[jax.ShapeDtypeStruct((batch * length, c), BF16)] * 2,
        scratch_shapes=[pltpu.VMEM((tm, c), F32)] * 2,
        compiler_params=_params(("parallel", "parallel", "arbitrary"), 48),
    )(cm, sm, vvb, kr, ki)


def _dft_inv_kernel(c_ref, s_ref, yr_ref, yi_ref, st_ref, vn_ref, vv_ref, x0_ref, d_ref, o_ref, acc,
                    *, tm, length):
    kt = pl.program_id(2)

    @pl.when(kt == 0)
    def _():
        acc[...] = jnp.zeros_like(acc)

    acc[...] += (jnp.dot(c_ref[...], yr_ref[...], preferred_element_type=F32)
                 - jnp.dot(s_ref[...], yi_ref[...], preferred_element_type=F32))

    @pl.when(kt == pl.num_programs(2) - 1)
    def _():
        norm = jnp.sum(st_ref[0], axis=0, keepdims=True)
        knyq = jnp.sum(st_ref[1], axis=0, keepdims=True)
        vnyq = jnp.sum(vn_ref[...], axis=0, keepdims=True)
        ynyq = vnyq * knyq / (2.0 * length * norm)
        row = lax.broadcasted_iota(jnp.int32, (tm, 1), 0)
        sign = jnp.where((row & 1) == 0, 1.0, -1.0)
        y = acc[...] + sign * ynyq
        o_ref[...] = ((y + vv_ref[...] * d_ref[...]) * x0_ref[...]).astype(BF16)


def _dft_inv(cm, sm, yr, yi, st, vnyq, vvf, x0, d, row0, batch, tm, tk):
    length = cm.shape[0]
    c = HYENA_WIDTH
    nm, nk = length // tm, length // tk
    off_m = row0 // tm
    seq0 = 0 if row0 == 0 else 1
    return pl.pallas_call(
        functools.partial(_dft_inv_kernel, tm=tm, length=length),
        grid=(nm, batch, nk),
        in_specs=[
            pl.BlockSpec((tm, tk), lambda m, b, k: (m, k)),
            pl.BlockSpec((tm, tk), lambda m, b, k: (m, k)),
            pl.BlockSpec((tk, c), lambda m, b, k: (b * nk + k, 0)),
            pl.BlockSpec((tk, c), lambda m, b, k: (b * nk + k, 0)),
            pl.BlockSpec((2, 8, c), lambda m, b, k: (0, 0, 0)),
            pl.BlockSpec((None, 8, c), lambda m, b, k: (seq0 + b, 0, 0)),
            pl.BlockSpec((tm, c), lambda m, b, k: (off_m + b * nm + m, 0)),
            pl.BlockSpec((tm, c), lambda m, b, k: (off_m + b * nm + m, 0)),
            pl.BlockSpec((1, c), lambda m, b, k: (0, 0)),
        ],
        out_specs=pl.BlockSpec((tm, c), lambda m, b, k: (b * nm + m, 0)),
        out_shape=jax.ShapeDtypeStruct((batch * length, c), BF16),
        scratch_shapes=[pltpu.VMEM((tm, c), F32)],
        compiler_params=_params(("parallel", "parallel", "arbitrary"), 48),
    )(cm, sm, yr, yi, st, vnyq, vvf, x0, d.reshape(1, c))


def _pack_pair(re, im):
    hi = lax.bitcast_convert_type(re.astype(BF16).astype(F32), jnp.uint32)
    lo = lax.bitcast_convert_type(im.astype(BF16).astype(F32), jnp.uint32)
    return (hi & jnp.uint32(0xFFFF0000)) | (lo >> 16)


def _unpack_stacked(p):
    re = lax.bitcast_convert_type(p & jnp.uint32(0xFFFF0000), F32)
    im = lax.bitcast_convert_type(p << 16, F32)
    return jnp.concatenate([re, im], axis=0).astype(BF16)


def _two_stage_tables(length):
    n = 2 * length
    n1 = n // DFT_INNER
    a = jnp.arange(n1, dtype=jnp.int32)
    ang1 = ((a[:, None] * a[None, :n1 // 2]) % n1).astype(F32) * (2.0 * math.pi / n1)
    c1, s1 = jnp.cos(ang1), jnp.sin(ang1)
    f1 = jnp.concatenate([c1, -s1], axis=0).astype(BF16)
    f3 = jnp.concatenate([c1.T, -s1.T], axis=1).astype(BF16)
    b = jnp.arange(DFT_INNER, dtype=jnp.int32)
    k = a[:, None, None] + n1 * b[None, :, None]
    ang2 = ((k * b[None, None, :]) % n).astype(F32) * (2.0 * math.pi / n)
    c2, s2 = jnp.cos(ang2), jnp.sin(ang2)
    m1 = jnp.concatenate([jnp.concatenate([c2, s2], axis=2), jnp.concatenate([-s2, c2], axis=2)], axis=1)
    return f1, f3, m1.astype(BF16), jnp.swapaxes(m1, 1, 2).astype(BF16)


def _s1_kernel(f_ref, x_ref, a_ref, *, tn):
    n1 = f_ref.shape[0] // 2
    for j in range(tn):
        a = jnp.dot(f_ref[...], x_ref[j].astype(BF16), preferred_element_type=F32)
        a_ref[:, j, :] = _pack_pair(a[:n1], a[n1:])


def _stage1(f1, x3, tn):
    rows2, half = f1.shape
    c = x3.shape[2]
    return pl.pallas_call(
        functools.partial(_s1_kernel, tn=tn),
        grid=(DFT_INNER // tn,),
        in_specs=[pl.BlockSpec((rows2, half), lambda i: (0, 0)), pl.BlockSpec((tn, half, c), lambda i: (i, 0, 0))],
        out_specs=pl.BlockSpec((rows2 // 2, tn, c), lambda i: (0, i, 0)),
        out_shape=jax.ShapeDtypeStruct((rows2 // 2, DFT_INNER, c), jnp.uint32),
        compiler_params=_params(("parallel",), 40),
    )(f1, x3)


def _mid_filter_kernel(e_ref, o_ref, m1_ref, st_ref, k_ref, *, kb, length):
    scale = 1.0 / (2.0 * length * jnp.sum(st_ref[0], axis=0, keepdims=True))
    for q in range(kb):
        xe = jnp.dot(m1_ref[q], _unpack_stacked(e_ref[q]), preferred_element_type=F32)
        xo = jnp.dot(m1_ref[q], _unpack_stacked(o_ref[q]), preferred_element_type=F32)
        k_ref[q] = _pack_pair(xe[:DFT_INNER] * scale, xo[DFT_INNER:] * scale)


def _mid_filter(ae, ao, m1, st, kb, ct):
    n1 = m1.shape[0]
    c = HYENA_WIDTH
    blk = pl.BlockSpec((kb, DFT_INNER, ct), lambda i, j: (i, 0, j))
    return pl.pallas_call(
        functools.partial(_mid_filter_kernel, kb=kb, length=n1 * DFT_INNER // 2),
        grid=(n1 // kb, c // ct),
        in_specs=[blk, blk,
                  pl.BlockSpec((kb, 2 * DFT_INNER, 2 * DFT_INNER), lambda i, j: (i, 0, 0)),
                  pl.BlockSpec((2, 8, ct), lambda i, j: (0, 0, j))],
        out_specs=blk,
        out_shape=jax.ShapeDtypeStruct((n1, DFT_INNER, c), jnp.uint32),
        compiler_params=_params(("parallel", "parallel"), 40),
    )(ae, ao, m1, st)


def _mid_kernel(a_ref, m1_ref, m2_ref, k_ref, z_ref, *, kb):
    for q in range(kb):
        x = jnp.dot(m1_ref[q], _unpack_stacked(a_ref[q]), preferred_element_type=F32)
        xr, xi = x[:DFT_INNER], x[DFT_INNER:]
        kq = k_ref[q]
        kr = lax.bitcast_convert_type(kq & jnp.uint32(0xFFFF0000), F32)
        ki = lax.bitcast_convert_type(kq << 16, F32)
        y = jnp.concatenate([xr * kr - xi * ki, xr * ki + xi * kr], axis=0).astype(BF16)
        z = jnp.dot(m2_ref[q], y, preferred_element_type=F32)
        z_ref[:, q, :] = _pack_pair(z[:DFT_INNER], z[DFT_INNER:])


def _mid(a, m1, m2, kf, kb, ct):
    n1 = m1.shape[0]
    c = HYENA_WIDTH
    mat = pl.BlockSpec((kb, 2 * DFT_INNER, 2 * DFT_INNER), lambda i, j: (i, 0, 0))
    return pl.pallas_call(
        functools.partial(_mid_kernel, kb=kb),
        grid=(n1 // kb, c // ct),
        in_specs=[pl.BlockSpec((kb, DFT_INNER, ct), lambda i, j: (i, 0, j)), mat, mat,
                  pl.BlockSpec((kb, DFT_INNER, ct), lambda i, j: (i, 0, j))],
        out_specs=pl.BlockSpec((DFT_INNER, kb, ct), lambda i, j: (0, i, j)),
        out_shape=jax.ShapeDtypeStruct((DFT_INNER, n1, c), jnp.uint32),
        compiler_params=_params(("parallel", "parallel"), 40),
    )(a, m1, m2, kf)


def _s3_kernel(f_ref, z_ref, y_ref, *, tn):
    for j in range(tn):
        y_ref[:, j, :] = jnp.dot(f_ref[...], _unpack_stacked(z_ref[j]), preferred_element_type=F32)


def _stage3(f3, z, tn):
    half = f3.shape[0]
    n1, c = z.shape[1], z.shape[2]
    return pl.pallas_call(
        functools.partial(_s3_kernel, tn=tn),
        grid=(DFT_INNER // tn,),
        in_specs=[pl.BlockSpec((half, 2 * n1), lambda i: (0, 0)), pl.BlockSpec((tn, n1, c), lambda i: (i, 0, 0))],
        out_specs=pl.BlockSpec((half, tn, c), lambda i: (0, i, 0)),
        out_shape=jax.ShapeDtypeStruct((half, DFT_INNER, c), F32),
        compiler_params=_params(("parallel",), 40),
    )(f3, z)


def _hy_out_kernel(y_ref, vv_ref, x0_ref, d_ref, o_ref):
    o_ref[...] = ((y_ref[...] + vv_ref[...] * d_ref[...]) * x0_ref[...]).astype(BF16)


def _hy_out(y, vvf, x0, d, tt):
    c = HYENA_WIDTH
    row = pl.BlockSpec((tt, c), lambda i: (i, 0))
    return pl.pallas_call(
        _hy_out_kernel,
        grid=(y.shape[0] // tt,),
        in_specs=[row, row, row, pl.BlockSpec((1, c), lambda i: (0, 0))],
        out_specs=row,
        out_shape=jax.ShapeDtypeStruct(y.shape, BF16),
        compiler_params=_params(("parallel",), 32),
    )(y, vvf, x0, d.reshape(1, c))


def _t5_buckets(rel):
    nb = N_BUCKETS // 2
    max_exact = nb // 2
    ret = (rel > 0).astype(jnp.int32) * nb
    n = jnp.abs(rel)
    large = max_exact + (jnp.log(jnp.maximum(n, 1).astype(F32) / max_exact)
                         / math.log(MAX_DISTANCE / max_exact) * (nb - max_exact)).astype(jnp.int32)
    large = jnp.minimum(large, nb - 1)
    return ret + jnp.where(n < max_exact, n, large)


def _bias_kernel(bk_ref, rb_ref, o_ref):
    h = pl.program_id(0)
    bk = bk_ref[...]
    qi = lax.broadcasted_iota(jnp.int32, bk.shape, 0)
    pj = lax.broadcasted_iota(jnp.int32, bk.shape, 1) - BLOCK
    acc = jnp.zeros(bk.shape, F32)
    for b in range(N_BUCKETS):
        acc = jnp.where(bk == b, rb_ref[b, h], acc)
    o_ref[...] = jnp.where(jnp.abs(pj - qi) <= WINDOW, acc, MASKED)


def _attn_bias(rel_bias):
    qi = jnp.arange(BLOCK)[:, None]
    pj = jnp.arange(3 * BLOCK)[None, :] - BLOCK
    buckets = _t5_buckets(pj - qi).astype(jnp.int32)
    return pl.pallas_call(
        _bias_kernel,
        grid=(N_HEADS,),
        in_specs=[
            pl.BlockSpec((BLOCK, 3 * BLOCK), lambda h: (0, 0)),
            pl.BlockSpec(memory_space=pltpu.SMEM),
        ],
        out_specs=pl.BlockSpec((None, BLOCK, 3 * BLOCK), lambda h: (h, 0, 0)),
        out_shape=jax.ShapeDtypeStruct((N_HEADS, BLOCK, 3 * BLOCK), F32),
        compiler_params=_params(("arbitrary",), 16),
    )(buckets, rel_bias.astype(F32))


def _attn_kernel(q_ref, kp_ref, kc_ref, kn_ref, vp_ref, vc_ref, vn_ref, bias_ref, sink_ref, o_ref, *, lp, ls):
    r0 = pl.program_id(0) * BLOCK
    start, length = _seq_info(r0, lp, ls)
    first = r0 == start
    last = r0 + BLOCK == start + length
    rows = GQA_GROUP * BLOCK
    col = lax.broadcasted_iota(jnp.int32, (rows, 3 * BLOCK), 1)
    outside = (first & (col < BLOCK)) | (last & (col >= 2 * BLOCK))
    grp = lax.broadcasted_iota(jnp.int32, (rows, 1), 0) // BLOCK
    scale = 1.0 / math.sqrt(HEAD_DIM)
    for kh in range(N_KV_HEADS):
        hs = slice(kh * HEAD_DIM, (kh + 1) * HEAD_DIM)
        kcat = jnp.concatenate([kp_ref[:, hs], kc_ref[:, hs], kn_ref[:, hs]], axis=0)
        vcat = jnp.concatenate([vp_ref[:, hs], vc_ref[:, hs], vn_ref[:, hs]], axis=0)
        heads = [kh * GQA_GROUP + g for g in range(GQA_GROUP)]
        q4 = jnp.concatenate([q_ref[:, h * HEAD_DIM:(h + 1) * HEAD_DIM] for h in heads], axis=0)
        s = lax.dot_general(q4, kcat, (((1,), (1,)), ((), ())), preferred_element_type=F32) * scale
        s = jnp.where(outside, MASKED, s + bias_ref[kh])
        sink = jnp.zeros((rows, 1), F32)
        for g, h in enumerate(heads):
            sink = jnp.where(grp == g, sink_ref[h], sink)
        m = jnp.maximum(jnp.max(s, axis=-1, keepdims=True), sink)
        p = jnp.exp(s - m)
        denom = jnp.sum(p, axis=-1, keepdims=True) + jnp.exp(sink - m)
        o = jnp.dot(p.astype(BF16), vcat, preferred_element_type=F32) / denom
        for g, h in enumerate(heads):
            o_ref[:, h * HEAD_DIM:(h + 1) * HEAD_DIM] = o[g * BLOCK:(g + 1) * BLOCK].astype(BF16)


def _attention(qkv, bias, sink, lp, ls):
    t = qkv.shape[0]
    nb = t // BLOCK
    qw = N_HEADS * HEAD_DIM
    kvw = N_KV_HEADS * HEAD_DIM
    kcol = qw // kvw
    prev = lambda i: jnp.maximum(i - 1, 0)
    nxt = lambda i: jnp.minimum(i + 1, nb - 1)
    kv_specs = [pl.BlockSpec((BLOCK, kvw), lambda i, f=f, cb=cb: (f(i), cb))
                for cb in (kcol, kcol + 1) for f in (prev, lambda i: i, nxt)]
    return pl.pallas_call(
        functools.partial(_attn_kernel, lp=lp, ls=ls),
        grid=(nb,),
        in_specs=[pl.BlockSpec((BLOCK, qw), lambda i: (i, 0))] + kv_specs + [
            pl.BlockSpec((N_KV_HEADS, GQA_GROUP * BLOCK, 3 * BLOCK), lambda i: (0, 0, 0)),
            pl.BlockSpec(memory_space=pltpu.SMEM),
        ],
        out_specs=pl.BlockSpec((BLOCK, qw), lambda i: (i, 0)),
        out_shape=jax.ShapeDtypeStruct((t, qw), BF16),
        compiler_params=_params(("parallel",), 32),
    )(qkv, *([qkv] * 6), bias.reshape(N_KV_HEADS, GQA_GROUP * BLOCK, 3 * BLOCK), sink.astype(F32))


def _tiles(lp, ls, dff):
    tok = min(512, ls)
    return dict(
        tm=tok,
        fm=min(1024, ls),
        tf=512 if dff % 512 == 0 else dff,
        tn=min(1024, ls),
        tt=min(512, ls),
        dm=min(512, ls),
        dk=min(2048, ls),
        tn2=8,
        kb=8,
        ct=512,
    )


def _forward(x_prompt, x_sample, p):
    depth = p['norm_ffn1'].shape[0]
    lp = x_prompt.shape[0] * x_prompt.shape[1]
    assert x_prompt.shape[0] == 1
    bs, ls = x_sample.shape[0], x_sample.shape[1]
    n_seq = 1 + bs
    assert lp % ls == 0
    dff = p['ffn1_wo'].shape[1]
    tl = _tiles(lp, ls, dff)
    tm, tt = tl['tm'], tl['tt']

    bf = lambda a: a.astype(BF16)
    ffn1_wi, ffn1_wo = bf(p['ffn1_wi']), bf(p['ffn1_wo'])
    ffn2_wi, ffn2_wo = bf(p['ffn2_wi']), bf(p['ffn2_wo'])
    ab_w_in, ab_w_out, pool_w = bf(p['ab_w_in']), bf(p['ab_w_out']), bf(p['pool_w'])
    w_qkv, w_o = bf(p['attn_w_qkv']), bf(p['attn_w_o'])
    bias = _attn_bias(p['rel_bias'])
    segments = ((0, 1, lp), (lp, bs, ls))
    lengths = {lp, ls}
    tables = {length: _two_stage_tables(length) for length in lengths if length >= TWO_STAGE_MIN_LEN}
    mats = {length: _dft_mats(length) for length in lengths if length < TWO_STAGE_MIN_LEN}

    for layer in range(depth):
        pieces = (x_prompt.reshape(lp, D_MODEL), x_sample.reshape(bs * ls, D_MODEL)) if layer == 0 else (x,)
        xs = tuple(_ffn(a, p['norm_ffn1'][layer], ffn1_wi, ffn1_wo, layer, tl['fm'], tl['tf']) for a in pieces)
        i = layer // 2
        if layer % 2 == 0:
            u = _norm_matmul(xs, p['norm_mix'][layer], ab_w_in, i, tl['fm'] // len(xs), tl['tn'])
            ya = _pool(u, pool_w[i], p['pool_scale'][i], tt, lp, ls)
            vvb, vvf, x0, vnyq, *vv_slabs = _gate(u, p['hy_conv_w'][i].astype(F32), p['hy_conv_b'][i].astype(F32),
                                                  tt, lp, ls, n_seq, lp if lp in tables else 0)
            fp = _pad_filter_params(p['hy_ff_w1'][i], p['hy_ff_b1'][i], p['hy_ff_w2'][i], p['hy_ff_b2'][i],
                                    p['hy_ff_w3'][i], p['hy_ff_b3'][i], p['hy_ff_w4'][i], p['hy_ff_b4'][i],
                                    p['hy_freq'][i])
            d = p['hy_d'][i].astype(F32)
            yb = []
            for row0, batch, length in segments:
                two_stage = length in tables
                e, o, st = _hyena_filter(length, fp, min(1024 if two_stage else 512, length), two_stage)
                if two_stage:
                    assert row0 == 0 and batch == 1
                    f1, f3, m1, m2 = tables[length]
                    tn, kb, ct = tl['tn2'], tl['kb'], tl['ct']
                    kf = _mid_filter(_stage1(f1, e, tn), _stage1(f1, o, tn), m1, st, kb, ct)
                    z = _mid(_stage1(f1, vv_slabs[0], tn), m1, m2, kf, kb, ct)
                    y = _stage3(f3, z, tn).reshape(length, HYENA_WIDTH)
                    yb.append(_hy_out(y, vvf, x0, d, tt))
                else:
                    cm, sm = mats[length]
                    dm, dk = min(tl['dm'], length), min(tl['dk'], length)
                    kr, ki = _dft_filter(cm, sm, e, o, st, dm, dk)
                    yr, yi = _dft_fwd(cm, sm, vvb, kr, ki, row0, batch, dm, dk)
                    yb.append(_dft_inv(cm, sm, yr, yi, st, vnyq, vvf, x0, d, row0, batch, dm, dk))
            x = _proj_residual(xs, [(ya,), tuple(yb)], ab_w_out, i, tm)
        else:
            qkv = _norm_matmul(xs, p['norm_mix'][layer], w_qkv, i, tl['fm'], tl['tn'])
            o = _attention(qkv, bias, p['attn_sink'][i], lp, ls)
            x = _proj_residual(xs, [(o,)], w_o, i, tm)
        ffn2 = functools.partial(_ffn, x, p['norm_ffn2'][layer], ffn2_wi, ffn2_wo, layer, tl['fm'], tl['tf'])
        if layer < depth - 1:
            x = ffn2()
    y_prompt = ffn2(rows=lp, final_g=p['norm_final'])
    y_sample = ffn2(in_row0=lp, rows=bs * ls, final_g=p['norm_final'])
    return y_prompt.reshape(x_prompt.shape), y_sample.reshape(x_sample.shape)


def kernel(x_prompt, x_sample, norm_ffn1, ffn1_wi, ffn1_wo, norm_mix, ab_w_in, pool_w, pool_scale, hy_conv_w, hy_conv_b, hy_ff_w1, hy_ff_b1, hy_ff_w2, hy_ff_b2, hy_ff_w3, hy_ff_b3, hy_ff_w4, hy_ff_b4, hy_freq, hy_d, ab_w_out, attn_w_qkv, attn_w_o, attn_sink, rel_bias, norm_ffn2, ffn2_wi, ffn2_wo, norm_final):
    p = dict(norm_ffn1=norm_ffn1, ffn1_wi=ffn1_wi, ffn1_wo=ffn1_wo, norm_mix=norm_mix,
             ab_w_in=ab_w_in, pool_w=pool_w, pool_scale=pool_scale,
             hy_conv_w=hy_conv_w, hy_conv_b=hy_conv_b, hy_ff_w1=hy_ff_w1, hy_ff_b1=hy_ff_b1,
             hy_ff_w2=hy_ff_w2, hy_ff_b2=hy_ff_b2, hy_ff_w3=hy_ff_w3, hy_ff_b3=hy_ff_b3,
             hy_ff_w4=hy_ff_w4, hy_ff_b4=hy_ff_b4, hy_freq=hy_freq, hy_d=hy_d, ab_w_out=ab_w_out,
             attn_w_qkv=attn_w_qkv, attn_w_o=attn_w_o, attn_sink=attn_sink, rel_bias=rel_bias,
             norm_ffn2=norm_ffn2, ffn2_wi=ffn2_wi, ffn2_wo=ffn2_wo, norm_final=norm_final)
    return _forward(x_prompt, x_sample, p)
```

```python
import functools
import math

import jax
import jax.numpy as jnp
import numpy as np
from jax import lax
from jax.experimental import pallas as pl
from jax.experimental.pallas import tpu as pltpu

F32 = jnp.float32
BF16 = jnp.bfloat16
HIGHEST = lax.Precision.HIGHEST

D_MODEL = 2048
EPS = 1e-6
POOL_WIDTH = 1024
POOL_WINDOWS = (2, 4, 8, 16)
POOL_GROUP = 256
HYENA_WIDTH = 1024
HYENA_EMB_BANDS = 16
HYENA_FAST_DECAY = 0.3
HYENA_SLOW_DECAY = 1.5
HYENA_TARGET = 1e-2
N_HEADS = 16
N_KV_HEADS = 4
HEAD_DIM = 128
GQA_GROUP = 4
WINDOW = 128
BLOCK = 128
N_BUCKETS = 32
MAX_DISTANCE = 128
MASKED = -1e30
HALO = 16
MIB = 1024 * 1024
FILTER_LANES = 64
DFT_INNER = 128
TWO_STAGE_MIN_LEN = 4096
SLAB_BLOCK = 8


def _params(sem, vmem_mib):
    return pltpu.CompilerParams(dimension_semantics=sem, vmem_limit_bytes=vmem_mib * MIB)


def _seq_info(r0, lp, ls):
    in_prompt = r0 < lp
    idx = jnp.maximum(r0 - lp, 0) // ls
    start = jnp.where(in_prompt, 0, lp + idx * ls)
    length = jnp.where(in_prompt, lp, ls)
    return start, length


def _rmsnorm(x, g):
    ms = jnp.mean(x * x, axis=-1, keepdims=True)
    return x * lax.rsqrt(ms + EPS) * g


def _ffn_kernel(*refs, final_norm):
    x_ref, g_ref, wg_ref, wu_ref, wo_ref = refs[:5]
    o_ref, h_ref = refs[-2:]
    j = pl.program_id(1)

    def branch(h):
        gate = jnp.dot(h, wg_ref[...], preferred_element_type=F32)
        up = jnp.dot(h, wu_ref[...], preferred_element_type=F32)
        act = (gate * jax.nn.sigmoid(gate) * (0.5 * up)).astype(BF16)
        return jnp.dot(act, wo_ref[...], preferred_element_type=F32)

    @pl.when(j == 0)
    def _():
        h = _rmsnorm(x_ref[...], g_ref[...]).astype(BF16)
        h_ref[...] = h
        o_ref[...] = x_ref[...] + branch(h)

    @pl.when(j > 0)
    def _():
        o_ref[...] += branch(h_ref[...])

    if final_norm:
        @pl.when(j == pl.num_programs(1) - 1)
        def _():
            o_ref[...] = _rmsnorm(o_ref[...], refs[5][...])


def _chunked_columns(wi, tf):
    depth, d, two_dff = wi.shape
    nj = two_dff // (2 * tf)
    return wi.astype(BF16).reshape(depth, d, 2, nj, tf).transpose(0, 3, 2, 1, 4)


def _ffn(x, g, wi, wo, layer, tm, *, in_row0=0, rows=None, final_g=None):
    d = x.shape[1]
    rows = x.shape[0] if rows is None else rows
    nj, tf = wi.shape[1], wi.shape[4]
    i0 = in_row0 // tm
    row_vec = pl.BlockSpec((1, d), lambda i, j: (0, 0))
    in_specs = [
        pl.BlockSpec((tm, d), lambda i, j: (i0 + i, 0)),
        row_vec,
        pl.BlockSpec((None, None, None, d, tf), lambda i, j: (layer, j, 0, 0, 0)),
        pl.BlockSpec((None, None, None, d, tf), lambda i, j: (layer, j, 1, 0, 0)),
        pl.BlockSpec((None, tf, d), lambda i, j: (layer, j, 0)),
    ]
    args = [x, g.reshape(1, d), wi, wi, wo]
    if final_g is not None:
        in_specs.append(row_vec)
        args.append(final_g.reshape(1, d))
    return pl.pallas_call(
        functools.partial(_ffn_kernel, final_norm=final_g is not None),
        grid=(rows // tm, nj),
        in_specs=in_specs,
        out_specs=pl.BlockSpec((tm, d), lambda i, j: (i, 0)),
        out_shape=jax.ShapeDtypeStruct((rows, d), F32),
        scratch_shapes=[pltpu.VMEM((tm, d), BF16)],
        compiler_params=_params(("parallel", "arbitrary"), 56),
    )(*args)


def _norm_matmul_kernel(*refs, seam):
    nx = 1 if seam is None else 2
    x_refs, (g_ref, w_ref, o_ref, h_ref) = refs[:nx], refs[nx:]
    j = pl.program_id(1)

    @pl.when(j == 0)
    def _():
        h = _rmsnorm(_stacked_tile(x_refs, seam), g_ref[...]).astype(BF16)
        h_ref[...] = h
        o_ref[...] = jnp.dot(h, w_ref[...], preferred_element_type=F32).astype(o_ref.dtype)

    @pl.when(j > 0)
    def _():
        o_ref[...] = jnp.dot(h_ref[...], w_ref[...], preferred_element_type=F32).astype(o_ref.dtype)


def _norm_matmul(xs, g, w, layer, tm, tn):
    t, d = sum(a.shape[0] for a in xs), xs[0].shape[1]
    n = w.shape[2]
    x_specs, seam = _stacked_specs(xs, tm)
    window_bytes = 2 * (len(xs) * tm * d * 4 + d * tn * 2 + tm * tn * 2) + tm * d * 2 + tm * tn * 4
    return pl.pallas_call(
        functools.partial(_norm_matmul_kernel, seam=seam),
        grid=(t // tm, n // tn),
        in_specs=x_specs + [
            pl.BlockSpec((1, d), lambda i, j: (0, 0)),
            pl.BlockSpec((None, d, tn), lambda i, j: (layer, 0, j)),
        ],
        out_specs=pl.BlockSpec((tm, tn), lambda i, j: (i, j)),
        out_shape=jax.ShapeDtypeStruct((t, n), BF16),
        scratch_shapes=[pltpu.VMEM((tm, d), BF16)],
        compiler_params=_params(("parallel", "arbitrary"), window_bytes // MIB + 8),
    )(*xs, g.reshape(1, d), w)


def _stacked_specs(pieces, tm):
    width = pieces[0].shape[1]
    if len(pieces) == 1:
        return [pl.BlockSpec((tm, width), lambda i, *_: (i, 0))], None
    n0 = pieces[0].shape[0] // tm
    return [pl.BlockSpec((tm, width), lambda i, *_: (jnp.minimum(i, n0 - 1), 0)),
            pl.BlockSpec((tm, width), lambda i, *_: (jnp.maximum(i - n0, 0), 0))], n0


def _stacked_tile(refs, seam):
    if seam is None:
        return refs[0][...]
    return jnp.where(pl.program_id(0) < seam, refs[0][...], refs[1][...])


def _proj_residual_kernel(*refs, x_seam, seams):
    nx = 1 if x_seam is None else 2
    o_ref = refs[-1]
    rest = list(refs[nx:-1])
    acc = _stacked_tile(refs[:nx], x_seam)
    for seam in seams:
        n = 1 if seam is None else 2
        y = _stacked_tile(rest[:n], seam)
        acc = acc + jnp.dot(y, rest[n][...], preferred_element_type=F32)
        rest = rest[n + 1:]
    o_ref[...] = acc


def _proj_residual(xs, ys, w, layer, tm):
    t, d = sum(a.shape[0] for a in xs), xs[0].shape[1]
    kw = ys[0][0].shape[1]
    in_specs, x_seam = _stacked_specs(xs, tm)
    args = list(xs)
    seams = []
    for k, pieces in enumerate(ys):
        specs, seam = _stacked_specs(pieces, tm)
        in_specs += specs + [pl.BlockSpec((None, kw, d), lambda i, k=k: (layer, k, 0))]
        args += list(pieces) + [w]
        seams.append(seam)
    return pl.pallas_call(
        functools.partial(_proj_residual_kernel, x_seam=x_seam, seams=tuple(seams)),
        grid=(t // tm,),
        in_specs=in_specs,
        out_specs=pl.BlockSpec((tm, d), lambda i: (i, 0)),
        out_shape=jax.ShapeDtypeStruct((t, d), F32),
        compiler_params=_params(("parallel",), 48),
    )(*args)


def _pool_kernel(um_ref, up_ref, un_ref, w_ref, sc_ref, o_ref, ext_ref, *, tt, lp, ls):
    r0 = pl.program_id(0) * tt
    start, length = _seq_info(r0, lp, ls)
    first = r0 == start
    last = r0 + tt == start + length
    ext_ref[0:HALO, :] = jnp.where(first, 0.0, up_ref[...].astype(F32))
    ext_ref[HALO:HALO + tt, :] = um_ref[...].astype(F32)
    ext_ref[HALO + tt:, :] = jnp.where(last, 0.0, un_ref[...].astype(F32))
    pos = (r0 - start) + lax.broadcasted_iota(jnp.int32, (tt, 1), 0)
    for g, w in enumerate(POOL_WINDOWS):
        cs = slice(g * POOL_GROUP, (g + 1) * POOL_GROUP)
        s = ext_ref[pl.ds(HALO - w // 2, tt), cs]
        for off in range(-w // 2 + 1, w // 2):
            s = s + ext_ref[pl.ds(HALO + off, tt), cs]
        cnt = (jnp.minimum(pos + w // 2, length) - jnp.maximum(pos - w // 2, 0)).astype(F32)
        p = s / cnt - ext_ref[pl.ds(HALO, tt), cs]
        y = jnp.dot(p.astype(BF16), w_ref[g], preferred_element_type=F32) * sc_ref[:, cs]
        o_ref[:, cs] = y.astype(BF16)


def _pool(u, pool_w, scale, tt, lp, ls):
    t = u.shape[0]
    nh = t // HALO
    hb = tt // HALO
    return pl.pallas_call(
        functools.partial(_pool_kernel, tt=tt, lp=lp, ls=ls),
        grid=(t // tt,),
        in_specs=[
            pl.BlockSpec((tt, POOL_WIDTH), lambda i: (i, 0)),
            pl.BlockSpec((HALO, POOL_WIDTH), lambda i: (jnp.maximum(i * hb - 1, 0), 0)),
            pl.BlockSpec((HALO, POOL_WIDTH), lambda i: (jnp.minimum((i + 1) * hb, nh - 1), 0)),
            pl.BlockSpec((len(POOL_WINDOWS), POOL_GROUP, POOL_GROUP), lambda i: (0, 0, 0)),
            pl.BlockSpec((1, POOL_WIDTH), lambda i: (0, 0)),
        ],
        out_specs=pl.BlockSpec((tt, POOL_WIDTH), lambda i: (i, 0)),
        out_shape=jax.ShapeDtypeStruct((t, POOL_WIDTH), BF16),
        scratch_shapes=[pltpu.VMEM((tt + 2 * HALO, POOL_WIDTH), F32)],
        compiler_params=_params(("parallel",), 32),
    )(u, u, u, pool_w, scale.reshape(1, POOL_WIDTH))


def _gate_kernel(m0, m1, m2, p0, p1, p2, n0, n1, n2, cw_ref, cb_ref,
                 vvb_ref, vvf_ref, x0_ref, nyq_ref, *slab_refs, tt, lp, ls, slab_rows):
    r0 = pl.program_id(0) * tt
    start, length = _seq_info(r0, lp, ls)
    first = r0 == start
    last = r0 + tt == start + length
    row = lax.broadcasted_iota(jnp.int32, (tt, 1), 0)

    def conv(m_ref, p_ref, n_ref, q):
        cs = slice(q * HYENA_WIDTH, (q + 1) * HYENA_WIDTH)
        m = m_ref[...].astype(F32)
        before = jnp.where(first, 0.0, p_ref[...].astype(F32)[HALO - 1:HALO, :])
        after = jnp.where(last, 0.0, n_ref[...].astype(F32)[0:1, :])
        up = jnp.where(row == 0, before, pltpu.roll(m, 1, 0))
        dn = jnp.where(row == tt - 1, after, pltpu.roll(m, tt - 1, 0))
        return up * cw_ref[0:1, cs] + m * cw_ref[1:2, cs] + dn * cw_ref[2:3, cs] + cb_ref[:, cs]

    x0 = conv(m0, p0, n0, 0)
    x1 = conv(m1, p1, n1, 1)
    v = conv(m2, p2, n2, 2)
    vv = v * x1
    vvb_ref[...] = vv.astype(BF16)
    vvf_ref[...] = vv
    x0_ref[...] = x0
    sign = jnp.where((row & 1) == 0, 1.0, -1.0)
    part = (vv * sign).reshape(tt // 8, 8, HYENA_WIDTH).sum(axis=0)

    @pl.when(first)
    def _():
        nyq_ref[...] = part

    @pl.when(jnp.logical_not(first))
    def _():
        nyq_ref[...] += part

    if slab_rows:
        per_tile = tt // DFT_INNER
        tile = pl.program_id(0)
        for phase in range(SLAB_BLOCK // per_tile):
            @pl.when((r0 < slab_rows) & (tile % (SLAB_BLOCK // per_tile) == phase))
            def _(phase=phase):
                for a in range(per_tile):
                    slab_refs[0][:, phase * per_tile + a, :] = vv[a * DFT_INNER:(a + 1) * DFT_INNER]


def _gate(u, conv_w, conv_b, tt, lp, ls, n_seq, slab_rows):
    t = u.shape[0]
    nh = t // HALO
    hb = tt // HALO
    c = HYENA_WIDTH

    def seq_of(i):
        r0 = i * tt
        return jnp.where(r0 < lp, 0, 1 + jnp.maximum(r0 - lp, 0) // ls)

    main = [pl.BlockSpec((tt, c), lambda i, q=q: (i, 1 + q)) for q in range(3)]
    prev = [pl.BlockSpec((HALO, c), lambda i, q=q: (jnp.maximum(i * hb - 1, 0), 1 + q)) for q in range(3)]
    nxt = [pl.BlockSpec((HALO, c), lambda i, q=q: (jnp.minimum((i + 1) * hb, nh - 1), 1 + q)) for q in range(3)]
    out_specs = [
        pl.BlockSpec((tt, c), lambda i: (i, 0)),
        pl.BlockSpec((tt, c), lambda i: (i, 0)),
        pl.BlockSpec((tt, c), lambda i: (i, 0)),
        pl.BlockSpec((None, 8, c), lambda i: (seq_of(i), 0, 0)),
    ]
    out_shape = [
        jax.ShapeDtypeStruct((t, c), BF16),
        jax.ShapeDtypeStruct((t, c), F32),
        jax.ShapeDtypeStruct((t, c), F32),
        jax.ShapeDtypeStruct((n_seq, 8, c), F32),
    ]
    if slab_rows:
        tiles_per_block = SLAB_BLOCK * DFT_INNER // tt
        last_block = slab_rows // (SLAB_BLOCK * DFT_INNER) - 1
        out_specs.append(pl.BlockSpec((DFT_INNER, SLAB_BLOCK, c),
                                      lambda i: (0, jnp.minimum(i // tiles_per_block, last_block), 0)))
        out_shape.append(jax.ShapeDtypeStruct((DFT_INNER, slab_rows // DFT_INNER, c), F32))
    return pl.pallas_call(
        functools.partial(_gate_kernel, tt=tt, lp=lp, ls=ls, slab_rows=slab_rows),
        grid=(t // tt,),
        in_specs=main + prev + nxt + [
            pl.BlockSpec((3, 3 * c), lambda i: (0, 0)),
            pl.BlockSpec((1, 3 * c), lambda i: (0, 0)),
        ],
        out_specs=out_specs,
        out_shape=out_shape,
        compiler_params=_params(("arbitrary",), 56),
    )(*([u] * 9), conv_w, conv_b.reshape(1, 3 * c))


def _filter_kernel(z_ref, w1, b1, w2, b2, w3, b3, w4ah, w4al, w4bh, w4bl, b4, fr_ref, dl_ref,
                   e_ref, o_ref, st_ref, *, tr, slab_major):
    i = pl.program_id(0)
    z = z_ref[...]
    fr = fr_ref[...]

    def dot(a, b):
        return jnp.dot(a, b, preferred_element_type=F32, precision=HIGHEST)

    def dot_bf16(a, b):
        return jnp.dot(a, b, preferred_element_type=F32)

    h = jnp.sin(fr * (dot(z, w1[...]) + b1[...]))
    h = jnp.sin(fr * (dot(h, w2[...]) + b2[...]))
    h = jnp.sin(fr * (dot(h, w3[...]) + b3[...]))
    hh_ = h.astype(BF16)
    hl_ = (h - hh_.astype(F32)).astype(BF16)

    def last(wh, wl):
        return (dot_bf16(hl_, wh[...]) + dot_bf16(hh_, wl[...])) + dot_bf16(hh_, wh[...]) + b4[...]

    hh = jnp.concatenate([last(w4ah, w4al), last(w4bh, w4bl)], axis=0)
    t = jnp.concatenate([z[:, 0:1], z[:, FILTER_LANES:FILTER_LANES + 1]], axis=0)
    decay = jnp.exp(-t * jnp.abs(dl_ref[...]))
    row = i * tr + lax.broadcasted_iota(jnp.int32, (tr, 1), 0)
    hf = hh[:, :HYENA_WIDTH] * decay
    hb = jnp.where(row == 0, 0.0, hh[:, HYENA_WIDTH:] * decay)
    e = hf + hb
    o = hf - hb
    if slab_major:
        for a in range(tr // DFT_INNER):
            e_ref[:, a, :] = e[a * DFT_INNER:(a + 1) * DFT_INNER].astype(e_ref.dtype)
            o_ref[:, a, :] = o[a * DFT_INNER:(a + 1) * DFT_INNER].astype(o_ref.dtype)
    else:
        e_ref[...] = e.astype(e_ref.dtype)
        o_ref[...] = o.astype(o_ref.dtype)
    sign = jnp.where((row & 1) == 0, 1.0, -1.0)
    norm_part = (jnp.abs(hf) + jnp.abs(hb)).reshape(tr // 8, 8, HYENA_WIDTH).sum(axis=0)
    nyq_part = (e * sign).reshape(tr // 8, 8, HYENA_WIDTH).sum(axis=0)

    @pl.when(i == 0)
    def _():
        st_ref[0] = norm_part
        st_ref[1] = nyq_part

    @pl.when(i > 0)
    def _():
        st_ref[0] += norm_part
        st_ref[1] += nyq_part


def _hyena_filter(length, fp, tr, slab_major):
    c = HYENA_WIDTH
    if slab_major:
        taps_spec = pl.BlockSpec((DFT_INNER, tr // DFT_INNER, c), lambda i: (0, i, 0))
        taps_shape = jax.ShapeDtypeStruct((DFT_INNER, length // DFT_INNER, c), F32)
    else:
        taps_spec = pl.BlockSpec((tr, c), lambda i: (i, 0))
        taps_shape = jax.ShapeDtypeStruct((length, c), BF16)
    t = jnp.linspace(0.0, 1.0, length, dtype=F32)[:, None]
    w = 2.0 * math.pi * jnp.arange(length, dtype=F32)[:, None] / length
    f = jnp.linspace(1e-4, HYENA_EMB_BANDS - 1, HYENA_EMB_BANDS, dtype=F32)[None, :]
    z = jnp.concatenate([t, jnp.cos(f * w), -jnp.sin(f * w)], axis=-1)
    z = jnp.pad(z, ((0, 0), (0, FILTER_LANES - z.shape[1])))
    z = z.reshape(length // tr, 2, tr // 2, FILTER_LANES).transpose(0, 2, 1, 3).reshape(length // 2, 2 * FILTER_LANES)
    min_decay = math.log(HYENA_TARGET) / HYENA_SLOW_DECAY
    max_decay = math.log(HYENA_TARGET) / HYENA_FAST_DECAY
    deltas = jnp.linspace(min_decay, max_decay, c, dtype=F32)[None, :]
    full = lambda shape: pl.BlockSpec(shape, lambda i: (0,) * len(shape))
    return pl.pallas_call(
        functools.partial(_filter_kernel, tr=tr, slab_major=slab_major),
        grid=(length // tr,),
        in_specs=([pl.BlockSpec((tr // 2, 2 * FILTER_LANES), lambda i: (i, 0))]
                  + [full(a.shape) for a in fp] + [full((1, c))]),
        out_specs=[taps_spec, taps_spec, full((2, 8, c))],
        out_shape=[taps_shape, taps_shape, jax.ShapeDtypeStruct((2, 8, c), F32)],
        compiler_params=_params(("arbitrary",), 56),
    )(z, *fp, deltas)


def _pad_filter_params(w1, b1, w2, b2, w3, b3, w4, b4, freq):
    hid = FILTER_LANES
    assert max(w1.shape + w2.shape) <= hid
    pad2 = lambda a, r, c: jnp.pad(a.astype(F32), ((0, r - a.shape[0]), (0, c - a.shape[1])))
    zero = jnp.zeros((hid, hid), F32)
    diag = lambda a: jnp.block([[pad2(a, hid, hid), zero], [zero, pad2(a, hid, hid)]])
    row = lambda a: jnp.tile(jnp.pad(a.astype(F32), (0, hid - a.shape[0])), 2).reshape(1, 2 * hid)
    w4p = pad2(w4, hid, w4.shape[1])
    zero4 = jnp.zeros_like(w4p)

    def hi_lo(w):
        hi = w.astype(BF16)
        return hi, (w - hi.astype(F32)).astype(BF16)

    return (diag(w1), row(b1), diag(w2), row(b2), diag(w3), row(b3),
            *hi_lo(jnp.concatenate([w4p, zero4], axis=0)), *hi_lo(jnp.concatenate([zero4, w4p], axis=0)),
            b4.astype(F32).reshape(1, -1), row(freq))


def _dft_mats(length):
    n = 2 * length
    k = jnp.arange(length, dtype=jnp.int32)[:, None]

    def table(cols):
        ang = ((k * cols[None, :]) % n).astype(F32) * (2.0 * math.pi / n)
        return jnp.cos(ang), jnp.sin(ang)

    ca, sa = table(jnp.arange(length // 128, dtype=jnp.int32) * 128)
    cb, sb = table(jnp.arange(128, dtype=jnp.int32))
    cos = ca[:, :, None] * cb[:, None, :] - sa[:, :, None] * sb[:, None, :]
    sin = sa[:, :, None] * cb[:, None, :] + ca[:, :, None] * sb[:, None, :]
    return cos.reshape(length, length).astype(BF16), sin.reshape(length, length).astype(BF16)


def _spectrum_scale(st_ref, k0, tm, length):
    norm = jnp.sum(st_ref[0], axis=0, keepdims=True)
    krow = k0 + lax.broadcasted_iota(jnp.int32, (tm, 1), 0)
    wk = jnp.where(krow == 0, 1.0, 2.0)
    return wk / (2.0 * length * norm)


def _dft_filter_kernel(c_ref, s_ref, e_ref, o_ref, st_ref, kr_ref, ki_ref, accr, acci, *, tm, length):
    kt = pl.program_id(1)

    @pl.when(kt == 0)
    def _():
        accr[...] = jnp.zeros_like(accr)
        acci[...] = jnp.zeros_like(acci)

    accr[...] += jnp.dot(c_ref[...], e_ref[...], preferred_element_type=F32)
    acci[...] += jnp.dot(s_ref[...], o_ref[...], preferred_element_type=F32)

    @pl.when(kt == pl.num_programs(1) - 1)
    def _():
        sc = _spectrum_scale(st_ref, pl.program_id(0) * tm, tm, length)
        kr_ref[...] = accr[...] * sc
        ki_ref[...] = -acci[...] * sc


def _dft_filter(cm, sm, e, o, st, tm, tk):
    length = cm.shape[0]
    c = HYENA_WIDTH
    return pl.pallas_call(
        functools.partial(_dft_filter_kernel, tm=tm, length=length),
        grid=(length // tm, length // tk),
        in_specs=[
            pl.BlockSpec((tm, tk), lambda m, k: (m, k)),
            pl.BlockSpec((tm, tk), lambda m, k: (m, k)),
            pl.BlockSpec((tk, c), lambda m, k: (k, 0)),
            pl.BlockSpec((tk, c), lambda m, k: (k, 0)),
            pl.BlockSpec((2, 8, c), lambda m, k: (0, 0, 0)),
        ],
        out_specs=[pl.BlockSpec((tm, c), lambda m, k: (m, 0))] * 2,
        out_shape=[jax.ShapeDtypeStruct((length, c), F32)] * 2,
        scratch_shapes=[pltpu.VMEM((tm, c), F32)] * 2,
        compiler_params=_params(("parallel", "arbitrary"), 48),
    )(cm, sm, e, o, st)


def _dft_fwd_kernel(c_ref, s_ref, v_ref, kr_ref, ki_ref, yr_ref, yi_ref, accr, acci):
    kt = pl.program_id(2)

    @pl.when(kt == 0)
    def _():
        accr[...] = jnp.zeros_like(accr)
        acci[...] = jnp.zeros_like(acci)

    v = v_ref[...]
    accr[...] += jnp.dot(c_ref[...], v, preferred_element_type=F32)
    acci[...] += jnp.dot(s_ref[...], v, preferred_element_type=F32)

    @pl.when(kt == pl.num_programs(2) - 1)
    def _():
        vr = accr[...]
        vi = -acci[...]
        kr = kr_ref[...]
        ki = ki_ref[...]
        yr_ref[...] = (vr * kr - vi * ki).astype(BF16)
        yi_ref[...] = (vr * ki + vi * kr).astype(BF16)


def _dft_fwd(cm, sm, vvb, kr, ki, row0, batch, tm, tk):
    length = cm.shape[0]
    c = HYENA_WIDTH
    nm, nk = length // tm, length // tk
    off_k = row0 // tk
    return pl.pallas_call(
        _dft_fwd_kernel,
        grid=(nm, batch, nk),
        in_specs=[
            pl.BlockSpec((tm, tk), lambda m, b, k: (m, k)),
            pl.BlockSpec((tm, tk), lambda m, b, k: (m, k)),
            pl.BlockSpec((tk, c), lambda m, b, k: (off_k + b * nk + k, 0)),
            pl.BlockSpec((tm, c), lambda m, b, k: (m, 0)),
            pl.BlockSpec((tm, c), lambda m, b, k: (m, 0)),
        ],
        out_specs=[pl.BlockSpec((tm, c), lambda m, b, k: (b * nm + m, 0))] * 2,
        out_shape=[jax.ShapeDtypeStruct((batch * length, c), BF16)] * 2,
        scratch_shapes=[pltpu.VMEM((tm, c), F32)] * 2,
        compiler_params=_params(("parallel", "parallel", "arbitrary"), 48),
    )(cm, sm, vvb, kr, ki)


def _dft_inv_kernel(c_ref, s_ref, yr_ref, yi_ref, st_ref, vn_ref, vv_ref, x0_ref, d_ref, o_ref, acc,
                    *, tm, length):
    kt = pl.program_id(2)

    @pl.when(kt == 0)
    def _():
        acc[...] = jnp.zeros_like(acc)

    acc[...] += (jnp.dot(c_ref[...], yr_ref[...], preferred_element_type=F32)
                 - jnp.dot(s_ref[...], yi_ref[...], preferred_element_type=F32))

    @pl.when(kt == pl.num_programs(2) - 1)
    def _():
        norm = jnp.sum(st_ref[0], axis=0, keepdims=True)
        knyq = jnp.sum(st_ref[1], axis=0, keepdims=True)
        vnyq = jnp.sum(vn_ref[...], axis=0, keepdims=True)
        ynyq = vnyq * knyq / (2.0 * length * norm)
        row = lax.broadcasted_iota(jnp.int32, (tm, 1), 0)
        sign = jnp.where((row & 1) == 0, 1.0, -1.0)
        y = acc[...] + sign * ynyq
        o_ref[...] = ((y + vv_ref[...] * d_ref[...]) * x0_ref[...]).astype(BF16)


def _dft_inv(cm, sm, yr, yi, st, vnyq, vvf, x0, d, row0, batch, tm, tk):
    length = cm.shape[0]
    c = HYENA_WIDTH
    nm, nk = length // tm, length // tk
    off_m = row0 // tm
    seq0 = 0 if row0 == 0 else 1
    return pl.pallas_call(
        functools.partial(_dft_inv_kernel, tm=tm, length=length),
        grid=(nm, batch, nk),
        in_specs=[
            pl.BlockSpec((tm, tk), lambda m, b, k: (m, k)),
            pl.BlockSpec((tm, tk), lambda m, b, k: (m, k)),
            pl.BlockSpec((tk, c), lambda m, b, k: (b * nk + k, 0)),
            pl.BlockSpec((tk, c), lambda m, b, k: (b * nk + k, 0)),
            pl.BlockSpec((2, 8, c), lambda m, b, k: (0, 0, 0)),
            pl.BlockSpec((None, 8, c), lambda m, b, k: (seq0 + b, 0, 0)),
            pl.BlockSpec((tm, c), lambda m, b, k: (off_m + b * nm + m, 0)),
            pl.BlockSpec((tm, c), lambda m, b, k: (off_m + b * nm + m, 0)),
            pl.BlockSpec((1, c), lambda m, b, k: (0, 0)),
        ],
        out_specs=pl.BlockSpec((tm, c), lambda m, b, k: (b * nm + m, 0)),
        out_shape=jax.ShapeDtypeStruct((batch * length, c), BF16),
        scratch_shapes=[pltpu.VMEM((tm, c), F32)],
        compiler_params=_params(("parallel", "parallel", "arbitrary"), 48),
    )(cm, sm, yr, yi, st, vnyq, vvf, x0, d.reshape(1, c))


def _pack_pair(re, im):
    hi = lax.bitcast_convert_type(re.astype(BF16).astype(F32), jnp.uint32)
    lo = lax.bitcast_convert_type(im.astype(BF16).astype(F32), jnp.uint32)
    return (hi & jnp.uint32(0xFFFF0000)) | (lo >> 16)


def _unpack_stacked(p):
    re = lax.bitcast_convert_type(p & jnp.uint32(0xFFFF0000), F32)
    im = lax.bitcast_convert_type(p << 16, F32)
    return jnp.concatenate([re, im], axis=0).astype(BF16)


def _two_stage_tables(length):
    n = 2 * length
    n1 = n // DFT_INNER
    a = jnp.arange(n1, dtype=jnp.int32)
    ang1 = ((a[:, None] * a[None, :n1 // 2]) % n1).astype(F32) * (2.0 * math.pi / n1)
    c1, s1 = jnp.cos(ang1), jnp.sin(ang1)
    f1 = jnp.concatenate([c1, -s1], axis=0).astype(BF16)
    f3 = jnp.concatenate([c1.T, -s1.T], axis=1).astype(BF16)
    b = jnp.arange(DFT_INNER, dtype=jnp.int32)
    k = a[:, None, None] + n1 * b[None, :, None]
    ang2 = ((k * b[None, None, :]) % n).astype(F32) * (2.0 * math.pi / n)
    c2, s2 = jnp.cos(ang2), jnp.sin(ang2)
    m1 = jnp.concatenate([jnp.concatenate([c2, s2], axis=2), jnp.concatenate([-s2, c2], axis=2)], axis=1)
    return f1, f3, m1.astype(BF16), jnp.swapaxes(m1, 1, 2).astype(BF16)


def _s1_kernel(f_ref, x_ref, a_ref, *, tn):
    n1 = f_ref.shape[0] // 2
    for j in range(tn):
        a = jnp.dot(f_ref[...], x_ref[j].astype(BF16), preferred_element_type=F32)
        a_ref[:, j, :] = _pack_pair(a[:n1], a[n1:])


def _stage1(f1, x3, tn):
    rows2, half = f1.shape
    c = x3.shape[2]
    return pl.pallas_call(
        functools.partial(_s1_kernel, tn=tn),
        grid=(DFT_INNER // tn,),
        in_specs=[pl.BlockSpec((rows2, half), lambda i: (0, 0)), pl.BlockSpec((tn, half, c), lambda i: (i, 0, 0))],
        out_specs=pl.BlockSpec((rows2 // 2, tn, c), lambda i: (0, i, 0)),
        out_shape=jax.ShapeDtypeStruct((rows2 // 2, DFT_INNER, c), jnp.uint32),
        compiler_params=_params(("parallel",), 40),
    )(f1, x3)


def _mid_filter_kernel(e_ref, o_ref, m1_ref, st_ref, k_ref, *, kb, length):
    scale = 1.0 / (2.0 * length * jnp.sum(st_ref[0], axis=0, keepdims=True))
    for q in range(kb):
        xe = jnp.dot(m1_ref[q], _unpack_stacked(e_ref[q]), preferred_element_type=F32)
        xo = jnp.dot(m1_ref[q], _unpack_stacked(o_ref[q]), preferred_element_type=F32)
        k_ref[q] = _pack_pair(xe[:DFT_INNER] * scale, xo[DFT_INNER:] * scale)


def _mid_filter(ae, ao, m1, st, kb, ct):
    n1 = m1.shape[0]
    c = HYENA_WIDTH
    blk = pl.BlockSpec((kb, DFT_INNER, ct), lambda i, j: (i, 0, j))
    return pl.pallas_call(
        functools.partial(_mid_filter_kernel, kb=kb, length=n1 * DFT_INNER // 2),
        grid=(n1 // kb, c // ct),
        in_specs=[blk, blk,
                  pl.BlockSpec((kb, 2 * DFT_INNER, 2 * DFT_INNER), lambda i, j: (i, 0, 0)),
                  pl.BlockSpec((2, 8, ct), lambda i, j: (0, 0, j))],
        out_specs=blk,
        out_shape=jax.ShapeDtypeStruct((n1, DFT_INNER, c), jnp.uint32),
        compiler_params=_params(("parallel", "parallel"), 40),
    )(ae, ao, m1, st)


def _mid_kernel(a_ref, m1_ref, m2_ref, k_ref, z_ref, *, kb):
    for q in range(kb):
        x = jnp.dot(m1_ref[q], _unpack_stacked(a_ref[q]), preferred_element_type=F32)
        xr, xi = x[:DFT_INNER], x[DFT_INNER:]
        kq = k_ref[q]
        kr = lax.bitcast_convert_type(kq & jnp.uint32(0xFFFF0000), F32)
        ki = lax.bitcast_convert_type(kq << 16, F32)
        y = jnp.concatenate([xr * kr - xi * ki, xr * ki + xi * kr], axis=0).astype(BF16)
        z = jnp.dot(m2_ref[q], y, preferred_element_type=F32)
        z_ref[:, q, :] = _pack_pair(z[:DFT_INNER], z[DFT_INNER:])


def _mid(a, m1, m2, kf, kb, ct):
    n1 = m1.shape[0]
    c = HYENA_WIDTH
    mat = pl.BlockSpec((kb, 2 * DFT_INNER, 2 * DFT_INNER), lambda i, j: (i, 0, 0))
    return pl.pallas_call(
        functools.partial(_mid_kernel, kb=kb),
        grid=(n1 // kb, c // ct),
        in_specs=[pl.BlockSpec((kb, DFT_INNER, ct), lambda i, j: (i, 0, j)), mat, mat,
                  pl.BlockSpec((kb, DFT_INNER, ct), lambda i, j: (i, 0, j))],
        out_specs=pl.BlockSpec((DFT_INNER, kb, ct), lambda i, j: (0, i, j)),
        out_shape=jax.ShapeDtypeStruct((DFT_INNER, n1, c), jnp.uint32),
        compiler_params=_params(("parallel", "parallel"), 40),
    )(a, m1, m2, kf)


def _s3_kernel(f_ref, z_ref, y_ref, *, tn):
    for j in range(tn):
        y_ref[:, j, :] = jnp.dot(f_ref[...], _unpack_stacked(z_ref[j]), preferred_element_type=F32)


def _stage3(f3, z, tn):
    half = f3.shape[0]
    n1, c = z.shape[1], z.shape[2]
    return pl.pallas_call(
        functools.partial(_s3_kernel, tn=tn),
        grid=(DFT_INNER // tn,),
        in_specs=[pl.BlockSpec((half, 2 * n1), lambda i: (0, 0)), pl.BlockSpec((tn, n1, c), lambda i: (i, 0, 0))],
        out_specs=pl.BlockSpec((half, tn, c), lambda i: (0, i, 0)),
        out_shape=jax.ShapeDtypeStruct((half, DFT_INNER, c), F32),
        compiler_params=_params(("parallel",), 40),
    )(f3, z)


def _hy_out_kernel(y_ref, vv_ref, x0_ref, d_ref, o_ref):
    o_ref[...] = ((y_ref[...] + vv_ref[...] * d_ref[...]) * x0_ref[...]).astype(BF16)


def _hy_out(y, vvf, x0, d, tt):
    c = HYENA_WIDTH
    row = pl.BlockSpec((tt, c), lambda i: (i, 0))
    return pl.pallas_call(
        _hy_out_kernel,
        grid=(y.shape[0] // tt,),
        in_specs=[row, row, row, pl.BlockSpec((1, c), lambda i: (0, 0))],
        out_specs=row,
        out_shape=jax.ShapeDtypeStruct(y.shape, BF16),
        compiler_params=_params(("parallel",), 32),
    )(y, vvf, x0, d.reshape(1, c))


def _t5_buckets(rel):
    nb = N_BUCKETS // 2
    max_exact = nb // 2
    ret = (rel > 0).astype(jnp.int32) * nb
    n = jnp.abs(rel)
    large = max_exact + (jnp.log(jnp.maximum(n, 1).astype(F32) / max_exact)
                         / math.log(MAX_DISTANCE / max_exact) * (nb - max_exact)).astype(jnp.int32)
    large = jnp.minimum(large, nb - 1)
    return ret + jnp.where(n < max_exact, n, large)


def _bias_kernel(bk_ref, rb_ref, o_ref):
    h = pl.program_id(0)
    bk = bk_ref[...]
    qi = lax.broadcasted_iota(jnp.int32, bk.shape, 0)
    pj = lax.broadcasted_iota(jnp.int32, bk.shape, 1) - BLOCK
    acc = jnp.zeros(bk.shape, F32)
    for b in range(N_BUCKETS):
        acc = jnp.where(bk == b, rb_ref[b, h], acc)
    o_ref[...] = jnp.where(jnp.abs(pj - qi) <= WINDOW, acc, MASKED)


def _attn_bias(rel_bias):
    qi = jnp.arange(BLOCK)[:, None]
    pj = jnp.arange(3 * BLOCK)[None, :] - BLOCK
    buckets = _t5_buckets(pj - qi).astype(jnp.int32)
    return pl.pallas_call(
        _bias_kernel,
        grid=(N_HEADS,),
        in_specs=[
            pl.BlockSpec((BLOCK, 3 * BLOCK), lambda h: (0, 0)),
            pl.BlockSpec(memory_space=pltpu.SMEM),
        ],
        out_specs=pl.BlockSpec((None, BLOCK, 3 * BLOCK), lambda h: (h, 0, 0)),
        out_shape=jax.ShapeDtypeStruct((N_HEADS, BLOCK, 3 * BLOCK), F32),
        compiler_params=_params(("arbitrary",), 16),
    )(buckets, rel_bias.astype(F32))


def _attn_kernel(q_ref, kp_ref, kc_ref, kn_ref, vp_ref, vc_ref, vn_ref, bias_ref, sink_ref, o_ref, *, lp, ls):
    r0 = pl.program_id(0) * BLOCK
    start, length = _seq_info(r0, lp, ls)
    first = r0 == start
    last = r0 + BLOCK == start + length
    rows = GQA_GROUP * BLOCK
    col = lax.broadcasted_iota(jnp.int32, (rows, 3 * BLOCK), 1)
    outside = (first & (col < BLOCK)) | (last & (col >= 2 * BLOCK))
    grp = lax.broadcasted_iota(jnp.int32, (rows, 1), 0) // BLOCK
    scale = 1.0 / math.sqrt(HEAD_DIM)
    for kh in range(N_KV_HEADS):
        hs = slice(kh * HEAD_DIM, (kh + 1) * HEAD_DIM)
        kcat = jnp.concatenate([kp_ref[:, hs], kc_ref[:, hs], kn_ref[:, hs]], axis=0)
        vcat = jnp.concatenate([vp_ref[:, hs], vc_ref[:, hs], vn_ref[:, hs]], axis=0)
        heads = [kh * GQA_GROUP + g for g in range(GQA_GROUP)]
        q4 = jnp.concatenate([q_ref[:, h * HEAD_DIM:(h + 1) * HEAD_DIM] for h in heads], axis=0)
        s = lax.dot_general(q4, kcat, (((1,), (1,)), ((), ())), preferred_element_type=F32) * scale
        s = jnp.where(outside, MASKED, s + bias_ref[kh])
        sink = jnp.zeros((rows, 1), F32)
        for g, h in enumerate(heads):
            sink = jnp.where(grp == g, sink_ref[h], sink)
        m = jnp.maximum(jnp.max(s, axis=-1, keepdims=True), sink)
        p = jnp.exp(s - m)
        denom = jnp.sum(p, axis=-1, keepdims=True) + jnp.exp(sink - m)
        o = jnp.dot(p.astype(BF16), vcat, preferred_element_type=F32) / denom
        for g, h in enumerate(heads):
            o_ref[:, h * HEAD_DIM:(h + 1) * HEAD_DIM] = o[g * BLOCK:(g + 1) * BLOCK].astype(BF16)


def _attention(qkv, bias, sink, lp, ls):
    t = qkv.shape[0]
    nb = t // BLOCK
    qw = N_HEADS * HEAD_DIM
    kvw = N_KV_HEADS * HEAD_DIM
    kcol = qw // kvw
    prev = lambda i: jnp.maximum(i - 1, 0)
    nxt = lambda i: jnp.minimum(i + 1, nb - 1)
    kv_specs = [pl.BlockSpec((BLOCK, kvw), lambda i, f=f, cb=cb: (f(i), cb))
                for cb in (kcol, kcol + 1) for f in (prev, lambda i: i, nxt)]
    return pl.pallas_call(
        functools.partial(_attn_kernel, lp=lp, ls=ls),
        grid=(nb,),
        in_specs=[pl.BlockSpec((BLOCK, qw), lambda i: (i, 0))] + kv_specs + [
            pl.BlockSpec((N_KV_HEADS, GQA_GROUP * BLOCK, 3 * BLOCK), lambda i: (0, 0, 0)),
            pl.BlockSpec(memory_space=pltpu.SMEM),
        ],
        out_specs=pl.BlockSpec((BLOCK, qw), lambda i: (i, 0)),
        out_shape=jax.ShapeDtypeStruct((t, qw), BF16),
        compiler_params=_params(("parallel",), 32),
    )(qkv, *([qkv] * 6), bias.reshape(N_KV_HEADS, GQA_GROUP * BLOCK, 3 * BLOCK), sink.astype(F32))


def _tiles(lp, ls, dff):
    tok = min(512, ls)
    return dict(
        tm=tok,
        fm=min(1024, ls),
        tf=512 if dff % 512 == 0 else dff,
        tn=min(1024, ls),
        tt=min(512, ls),
        dm=min(512, ls),
        dk=min(2048, ls),
        tn2=8,
        kb=8,
        ct=512,
    )


def _forward(x_prompt, x_sample, p):
    depth = p['norm_ffn1'].shape[0]
    lp = x_prompt.shape[0] * x_prompt.shape[1]
    assert x_prompt.shape[0] == 1
    bs, ls = x_sample.shape[0], x_sample.shape[1]
    n_seq = 1 + bs
    assert lp % ls == 0
    dff = p['ffn1_wo'].shape[1]
    tl = _tiles(lp, ls, dff)
    tm, tt = tl['tm'], tl['tt']

    bf = lambda a: a.astype(BF16)
    ffn1_wi, ffn1_wo = _chunked_columns(p['ffn1_wi'], tl['tf']), bf(p['ffn1_wo'])
    ffn2_wi, ffn2_wo = _chunked_columns(p['ffn2_wi'], tl['tf']), bf(p['ffn2_wo'])
    ab_w_in, ab_w_out, pool_w = bf(p['ab_w_in']), bf(p['ab_w_out']), bf(p['pool_w'])
    w_qkv, w_o = bf(p['attn_w_qkv']), bf(p['attn_w_o'])
    bias = _attn_bias(p['rel_bias'])
    segments = ((0, 1, lp), (lp, bs, ls))
    lengths = {lp, ls}
    tables = {length: _two_stage_tables(length) for length in lengths if length >= TWO_STAGE_MIN_LEN}
    mats = {length: _dft_mats(length) for length in lengths if length < TWO_STAGE_MIN_LEN}

    for layer in range(depth):
        pieces = (x_prompt.reshape(lp, D_MODEL), x_sample.reshape(bs * ls, D_MODEL)) if layer == 0 else (x,)
        xs = tuple(_ffn(a, p['norm_ffn1'][layer], ffn1_wi, ffn1_wo, layer, tl['fm']) for a in pieces)
        i = layer // 2
        if layer % 2 == 0:
            u = _norm_matmul(xs, p['norm_mix'][layer], ab_w_in, i, tl['fm'], tl['tn'])
            ya = _pool(u, pool_w[i], p['pool_scale'][i], tt, lp, ls)
            vvb, vvf, x0, vnyq, *vv_slabs = _gate(u, p['hy_conv_w'][i].astype(F32), p['hy_conv_b'][i].astype(F32),
                                                  tt, lp, ls, n_seq, lp if lp in tables else 0)
            fp = _pad_filter_params(p['hy_ff_w1'][i], p['hy_ff_b1'][i], p['hy_ff_w2'][i], p['hy_ff_b2'][i],
                                    p['hy_ff_w3'][i], p['hy_ff_b3'][i], p['hy_ff_w4'][i], p['hy_ff_b4'][i],
                                    p['hy_freq'][i])
            d = p['hy_d'][i].astype(F32)
            yb = []
            for row0, batch, length in segments:
                two_stage = length in tables
                e, o, st = _hyena_filter(length, fp, min(1024 if two_stage else 512, length), two_stage)
                if two_stage:
                    assert row0 == 0 and batch == 1
                    f1, f3, m1, m2 = tables[length]
                    tn, kb, ct = tl['tn2'], tl['kb'], tl['ct']
                    kf = _mid_filter(_stage1(f1, e, tn), _stage1(f1, o, tn), m1, st, kb, ct)
                    z = _mid(_stage1(f1, vv_slabs[0], tn), m1, m2, kf, kb, ct)
                    y = _stage3(f3, z, tn).reshape(length, HYENA_WIDTH)
                    yb.append(_hy_out(y, vvf, x0, d, tt))
                else:
                    cm, sm = mats[length]
                    dm, dk = min(tl['dm'], length), min(tl['dk'], length)
                    kr, ki = _dft_filter(cm, sm, e, o, st, dm, dk)
                    yr, yi = _dft_fwd(cm, sm, vvb, kr, ki, row0, batch, dm, dk)
                    yb.append(_dft_inv(cm, sm, yr, yi, st, vnyq, vvf, x0, d, row0, batch, dm, dk))
            x = _proj_residual(xs, [(ya,), tuple(yb)], ab_w_out, i, tm)
        else:
            qkv = _norm_matmul(xs, p['norm_mix'][layer], w_qkv, i, tl['fm'], tl['tn'])
            o = _attention(qkv, bias, p['attn_sink'][i], lp, ls)
            x = _proj_residual(xs, [(o,)], w_o, i, tm)
        ffn2 = functools.partial(_ffn, x, p['norm_ffn2'][layer], ffn2_wi, ffn2_wo, layer, tl['fm'])
        if layer < depth - 1:
            x = ffn2()
    y_prompt = ffn2(rows=lp, final_g=p['norm_final'])
    y_sample = ffn2(in_row0=lp, rows=bs * ls, final_g=p['norm_final'])
    return y_prompt.reshape(x_prompt.shape), y_sample.reshape(x_sample.shape)


def kernel(x_prompt, x_sample, norm_ffn1, ffn1_wi, ffn1_wo, norm_mix, ab_w_in, pool_w, pool_scale, hy_conv_w, hy_conv_b, hy_ff_w1, hy_ff_b1, hy_ff_w2, hy_ff_b2, hy_ff_w3, hy_ff_b3, hy_ff_w4, hy_ff_b4, hy_freq, hy_d, ab_w_out, attn_w_qkv, attn_w_o, attn_sink, rel_bias, norm_ffn2, ffn2_wi, ffn2_wo, norm_final):
    p = dict(norm_ffn1=norm_ffn1, ffn1_wi=ffn1_wi, ffn1_wo=ffn1_wo, norm_mix=norm_mix,
             ab_w_in=ab_w_in, pool_w=pool_w, pool_scale=pool_scale,
             hy_conv_w=hy_conv_w, hy_conv_b=hy_conv_b, hy_ff_w1=hy_ff_w1, hy_ff_b1=hy_ff_b1,
             hy_ff_w2=hy_ff_w2, hy_ff_b2=hy_ff_b2, hy_ff_w3=hy_ff_w3, hy_ff_b3=hy_ff_b3,
             hy_ff_w4=hy_ff_w4, hy_ff_b4=hy_ff_b4, hy_freq=hy_freq, hy_d=hy_d, ab_w_out=ab_w_out,
             attn_w_qkv=attn_w_qkv, attn_w_o=attn_w_o, attn_sink=attn_sink, rel_bias=rel_bias,
             norm_ffn2=norm_ffn2, ffn2_wi=ffn2_wi, ffn2_wo=ffn2_wo, norm_final=norm_final)
    return _forward(x_prompt, x_sample, p)
```
